```python
import jax, jax.numpy as jnp
from jax import lax
import numpy as np

D_MODEL = 1024
BATCH = 8
SEQ = 4096
DEPTH = 4

N_MIXERS = 2
EXPAND = 2
D_INNER = EXPAND * D_MODEL
CONV_WIDTH = 3
HEAD_DIM = 128
N_HEADS = D_INNER // HEAD_DIM
CHUNK = 32
EPS = 1e-6
LB_FLOOR = 1e-30
N_CONV_LAYERS = (DEPTH + N_MIXERS - 1) // N_MIXERS
N_HGRN_LAYERS = (DEPTH + N_MIXERS - 2) // N_MIXERS

kernel_name = "bidir_shortconv_hgrn2_interleaved_trunk"


def rms_norm(x, w):
    xf = x.astype(jnp.float32)
    xf = xf * lax.rsqrt(jnp.mean(xf * xf, axis=-1, keepdims=True) + EPS)
    return (xf * w.astype(jnp.float32)).astype(x.dtype)


def short_conv_mixer(h, w_in, conv_k, w_out):
    proj = h @ w_in
    b_gate, c_gate, u, z = jnp.split(proj, 4, axis=-1)
    v = c_gate * u
    v = lax.conv_general_dilated(
        v, conv_k[:, None, :].astype(v.dtype), window_strides=(1,),
        padding=((CONV_WIDTH // 2, CONV_WIDTH // 2),),
        dimension_numbers=("NWC", "WIO", "NWC"),
        feature_group_count=D_INNER)
    y = b_gate * v * jax.nn.silu(z)
    return y @ w_out


def gla_chunk_scan(q, k, v, log_f):
    bsz, nh, seq, dk = q.shape
    dv = v.shape[-1]
    n_chunks = seq // CHUNK

    def to_chunks(a):
        return jnp.moveaxis(a.reshape(bsz, nh, n_chunks, CHUNK, a.shape[-1]), 2, 0)

    causal_in_chunk = jnp.tril(jnp.ones((CHUNK, CHUNK), dtype=bool))[:, :, None]

    def step(state, inp):
        qc, kc, vc, gc = inp
        g_cum = jnp.cumsum(gc, axis=2)
        o_inter = jnp.einsum("bhtk,bhkv->bhtv", qc * jnp.exp(g_cum), state)
        diff = g_cum[:, :, :, None, :] - g_cum[:, :, None, :, :]
        decay = jnp.where(causal_in_chunk, jnp.exp(jnp.where(causal_in_chunk, diff, 0.0)), 0.0)
        scores = jnp.einsum("bhtsk,bhsk->bhts", qc[:, :, :, None, :] * decay, kc)
        o_intra = jnp.einsum("bhts,bhsv->bhtv", scores, vc)
        g_last = g_cum[:, :, -1:, :]
        new_state = (jnp.exp(g_last[:, :, 0, :])[..., None] * state
                     + jnp.einsum("bhsk,bhsv->bhkv", kc * jnp.exp(g_last - g_cum), vc))
        return new_state, o_inter + o_intra

    s0 = jnp.zeros((bsz, nh, dk, dv), jnp.float32)
    _, o = lax.scan(step, s0, (to_chunks(q), to_chunks(k), to_chunks(v), to_chunks(log_f)))
    return jnp.moveaxis(o, 0, 2).reshape(bsz, nh, seq, dv)


def hgrn_lower_bounds(lb_logits):
    p = jax.nn.softmax(lb_logits.astype(jnp.float32), axis=0)
    return jnp.cumsum(p, axis=0) - p[0]


def hgrn2_mixer(h, w_in, lb, norm_w, w_out):
    bsz, seq, _ = h.shape
    proj = h @ w_in
    q, f_fw, f_bw, i_val, z = jnp.split(proj, 5, axis=-1)
    lb = jnp.clip(lb, 0.0, 1.0 - 1e-6)
    log_lb = jnp.log(jnp.maximum(lb, LB_FLOOR))
    log_1m_lb = jnp.log1p(-lb)

    def heads(a):
        return a.astype(jnp.float32).reshape(bsz, seq, N_HEADS, HEAD_DIM).transpose(0, 2, 1, 3)

    def gates(f_pre):
        fp = f_pre.astype(jnp.float32)
        log_f = jnp.logaddexp(log_lb, log_1m_lb + jax.nn.log_sigmoid(fp))
        key = (1.0 - lb) * jax.nn.sigmoid(-fp)
        return heads(key), heads(log_f)

    qh = heads(q) * (HEAD_DIM ** -0.5)
    vh = heads(i_val)
    k_fw, lf_fw = gates(f_fw)
    k_bw, lf_bw = gates(f_bw)
    o_fw = gla_chunk_scan(qh, k_fw, vh, lf_fw)
    flip = lambda a: jnp.flip(a, axis=2)
    o_bw = flip(gla_chunk_scan(flip(qh), flip(k_bw), flip(vh), flip(lf_bw)))
    o = o_fw + o_bw
    o = o * lax.rsqrt(jnp.mean(o * o, axis=-1, keepdims=True) + EPS)
    o = o.transpose(0, 2, 1, 3).reshape(bsz, seq, D_INNER) * norm_w.astype(jnp.float32)
    y = o.astype(h.dtype) * jax.nn.silu(z)
    return y @ w_out


def setup_inputs(seed: int = 0) -> dict:
    key = jax.random.key(seed)
    ks = jax.random.split(key, 10)
    nrm = jax.random.normal
    f32 = jnp.float32
    x = nrm(ks[0], (BATCH, SEQ, D_MODEL), f32)
    norm_w = 1.0 + 0.02 * nrm(ks[1], (DEPTH, D_MODEL), f32)
    final_norm_w = 1.0 + 0.02 * nrm(ks[2], (D_MODEL,), f32)
    conv_w_in = nrm(ks[3], (N_CONV_LAYERS, D_MODEL, 4 * D_INNER), f32) * D_MODEL ** -0.5
    conv_kernel = nrm(ks[4], (N_CONV_LAYERS, CONV_WIDTH, D_INNER), f32) * CONV_WIDTH ** -0.5
    conv_w_out = nrm(ks[5], (N_CONV_LAYERS, D_INNER, D_MODEL), f32) * D_INNER ** -0.5
    hgrn_w_in = nrm(ks[6], (N_HGRN_LAYERS, D_MODEL, 5 * D_INNER), f32) * D_MODEL ** -0.5
    hgrn_lb_logits = 0.1 * nrm(ks[7], (N_HGRN_LAYERS, D_INNER), f32)
    hgrn_norm_w = 1.0 + 0.02 * nrm(ks[8], (N_HGRN_LAYERS, D_INNER), f32)
    hgrn_w_out = nrm(ks[9], (N_HGRN_LAYERS, D_INNER, D_MODEL), f32) * D_INNER ** -0.5
    return {"x": x, "norm_w": norm_w, "final_norm_w": final_norm_w,
            "conv_w_in": conv_w_in, "conv_kernel": conv_kernel, "conv_w_out": conv_w_out,
            "hgrn_w_in": hgrn_w_in, "hgrn_lb_logits": hgrn_lb_logits,
            "hgrn_norm_w": hgrn_norm_w, "hgrn_w_out": hgrn_w_out}


def reference(x, norm_w, final_norm_w, conv_w_in, conv_kernel, conv_w_out,
              hgrn_w_in, hgrn_lb_logits, hgrn_norm_w, hgrn_w_out):
    lower_bounds = hgrn_lower_bounds(hgrn_lb_logits)
    for layer in range(DEPTH):
        h = rms_norm(x, norm_w[layer])
        j = layer // N_MIXERS
        if layer % N_MIXERS == 0:
            y = short_conv_mixer(h, conv_w_in[j], conv_kernel[j], conv_w_out[j])
        else:
            y = hgrn2_mixer(h, hgrn_w_in[j], lower_bounds[j], hgrn_norm_w[j], hgrn_w_out[j])
        x = x + y
    return rms_norm(x, final_norm_w)
```

```python
import functools

import numpy as np
import jax
import jax.numpy as jnp
from jax import lax
from jax.experimental import pallas as pl
from jax.experimental.pallas import tpu as pltpu

HEAD_DIM = 128
EPS = 1e-6
LB_FLOOR = 1e-30
CONV_WIDTH = 3

V7X_LANES = 128
V7X_BF16_SUBLANES = 16
VMEM_LIMIT_BYTES = 56 * 1024 * 1024

SCAN_CHUNK = 64
SCAN_LEVELS = (32, 16, 8, 4, 2, 1)
SCAN_BCAST_LEVELS = 4

F32 = jnp.float32
BF16 = jnp.bfloat16


def _dot(a, b):
    return jnp.dot(a, b, preferred_element_type=F32)


def _dot_nt(a, b):
    return lax.dot_general(a, b, (((1,), (1,)), ((), ())), preferred_element_type=F32)


def _dot_tn(a, b):
    return lax.dot_general(a, b, (((0,), (0,)), ((), ())), preferred_element_type=F32)


def _rms_norm(xv, w):
    ms = jnp.mean(xv * xv, axis=-1, keepdims=True)
    return xv * lax.rsqrt(ms + EPS) * w


def _silu(z):
    return z * jax.nn.sigmoid(z)


def _const_spec(shape):
    nd = len(shape)
    return pl.BlockSpec(shape, lambda *_: (0,) * nd, pipeline_mode=pl.Buffered(1))


def _conv_layer_kernel(x_ref, xp_ref, xn_ref, nw_ref, win_ref, ck_ref, wout_ref, o_ref, h_scr,
                       *, tt, te, halo):
    i = pl.program_id(1)
    nt = pl.num_programs(1)
    d_inner = wout_ref.shape[0]
    nw = nw_ref[...]
    x = x_ref[0]
    h_scr[halo:halo + tt, :] = _rms_norm(x, nw).astype(BF16)
    h_scr[0:halo, :] = jnp.where(i > 0, _rms_norm(xp_ref[0], nw), 0.0).astype(BF16)
    h_scr[halo + tt:, :] = jnp.where(i < nt - 1, _rms_norm(xn_ref[0], nw), 0.0).astype(BF16)

    rows = tt + 2 * halo
    acc = jnp.zeros((tt, o_ref.shape[-1]), F32)
    for e in range(d_inner // te):
        def cols(g):
            return slice(g * d_inner + e * te, g * d_inner + (e + 1) * te)
        h_main = h_scr[halo:halo + tt, :]
        h_ext = h_scr[...]
        b_gate = _dot(h_main, win_ref[:, cols(0)])
        v = _dot(h_ext, win_ref[:, cols(1)]) * _dot(h_ext, win_ref[:, cols(2)])
        z = _dot(h_main, win_ref[:, cols(3)])
        ck = ck_ref[:, e * te:(e + 1) * te]
        v_prev = pltpu.roll(v, 1, 0)[halo:halo + tt]
        v_next = pltpu.roll(v, rows - 1, 0)[halo:halo + tt]
        conv = ck[0:1] * v_prev + ck[1:2] * v[halo:halo + tt] + ck[2:3] * v_next
        y = b_gate * conv * _silu(z)
        acc = acc + _dot(y.astype(BF16), wout_ref[e * te:(e + 1) * te, :])
    o_ref[0] = x + acc


def _conv_layer(x, norm_w, w_in, conv_k, w_out, *, tt=512, te=512):
    bsz, seq, d = x.shape
    d_inner = w_out.shape[0]
    halo = V7X_BF16_SUBLANES
    assert seq % tt == 0 and tt % halo == 0 and d_inner % te == 0
    nt = seq // tt
    hb = tt // halo
    last_hb = seq // halo - 1
    kern = functools.partial(_conv_layer_kernel, tt=tt, te=te, halo=halo)
    return pl.pallas_call(
        kern,
        name="conv_layer",
        grid=(bsz, nt),
        in_specs=[
            pl.BlockSpec((1, tt, d), lambda b, i: (b, i, 0)),
            pl.BlockSpec((1, halo, d), lambda b, i: (b, jnp.maximum(i * hb - 1, 0), 0)),
            pl.BlockSpec((1, halo, d), lambda b, i: (b, jnp.minimum((i + 1) * hb, last_hb), 0)),
            _const_spec((1, d)),
            _const_spec(w_in.shape),
            _const_spec(conv_k.shape),
            _const_spec(w_out.shape),
        ],
        out_specs=pl.BlockSpec((1, tt, d), lambda b, i: (b, i, 0)),
        out_shape=jax.ShapeDtypeStruct(x.shape, F32),
        scratch_shapes=[pltpu.VMEM((tt + 2 * halo, d), BF16)],
        compiler_params=pltpu.CompilerParams(
            dimension_semantics=("parallel", "arbitrary"), vmem_limit_bytes=VMEM_LIMIT_BYTES),
    )(x, x, x, norm_w.reshape(1, d), w_in, conv_k, w_out)


def _hgrn_in_kernel(x_ref, nw_ref, win_ref, lbc_ref, lbf_ref,
                    q_ref, v_ref, z_ref, kf_ref, kb_ref, lff_ref, lfb_ref):
    d_inner = z_ref.shape[-1]
    n_heads = d_inner // HEAD_DIM
    h = _rms_norm(x_ref[0], nw_ref[...]).astype(BF16)

    def proj(g):
        return _dot(h, win_ref[:, g * d_inner:(g + 1) * d_inner])

    def put_heads(ref, val):
        for hd in range(n_heads):
            ref[0, hd] = val[:, hd * HEAD_DIM:(hd + 1) * HEAD_DIM].astype(ref.dtype)

    put_heads(q_ref, proj(0) * (HEAD_DIM ** -0.5))
    one_minus_lb = 1.0 - lbc_ref[...]
    lb_floor = lbf_ref[...]
    for g, k_ref, lf_ref in ((1, kf_ref, lff_ref), (2, kb_ref, lfb_ref)):
        fp = proj(g)
        e = jnp.exp(-jnp.abs(fp))
        r = 1.0 / (1.0 + e)
        er = e * r
        pos = fp >= 0.0
        sig = jnp.where(pos, r, er)
        nsig = jnp.where(pos, er, r)
        put_heads(lf_ref, jnp.log(lb_floor + one_minus_lb * sig))
        put_heads(k_ref, one_minus_lb * nsig)
    put_heads(v_ref, proj(3))
    z_ref[0] = proj(4).astype(z_ref.dtype)


def _hgrn_in(x, norm_w, w_in, lb, *, tt=256):
    bsz, seq, d = x.shape
    d_inner = lb.shape[-1]
    n_heads = d_inner // HEAD_DIM
    assert seq % tt == 0
    lbc = jnp.clip(lb, 0.0, 1.0 - 1e-6).reshape(1, d_inner)
    lbf = jnp.maximum(lbc, LB_FLOOR)
    head_shape = (bsz, n_heads, seq, HEAD_DIM)
    head_spec = pl.BlockSpec((1, n_heads, tt, HEAD_DIM), lambda b, i: (b, 0, i, 0))
    return pl.pallas_call(
        _hgrn_in_kernel,
        name="hgrn_in",
        grid=(bsz, seq // tt),
        in_specs=[
            pl.BlockSpec((1, tt, d), lambda b, i: (b, i, 0)),
            _const_spec((1, d)),
            _const_spec(w_in.shape),
            _const_spec((1, d_inner)),
            _const_spec((1, d_inner)),
        ],
        out_specs=[
            head_spec, head_spec,
            pl.BlockSpec((1, tt, d_inner), lambda b, i: (b, i, 0)),
            head_spec, head_spec, head_spec, head_spec,
        ],
        out_shape=[
            jax.ShapeDtypeStruct(head_shape, BF16),
            jax.ShapeDtypeStruct(head_shape, BF16),
            jax.ShapeDtypeStruct((bsz, seq, d_inner), BF16),
            jax.ShapeDtypeStruct(head_shape, BF16),
            jax.ShapeDtypeStruct(head_shape, BF16),
            jax.ShapeDtypeStruct(head_shape, F32),
            jax.ShapeDtypeStruct(head_shape, F32),
        ],
        compiler_params=pltpu.CompilerParams(
            dimension_semantics=("parallel", "parallel"), vmem_limit_bytes=VMEM_LIMIT_BYTES),
    )(x, norm_w.reshape(1, d), w_in, lbc, lbf)


def _scan_constants():
    c = SCAN_CHUNK
    levels = SCAN_LEVELS
    nb = SCAN_BCAST_LEVELS
    tri = np.zeros((2, c, c), np.float32)
    qsel = np.zeros((2, len(levels), c, 1), np.float32)
    mask = np.zeros((2, len(levels) + 1, c, c), np.float32)
    coef = np.zeros((2, len(levels) - nb, 3, c, 1), np.float32)
    refs = [[[] for _ in range(nb)] for _ in range(2)]
    flip = lambda a: a[::-1, ::-1]
    tri[0] = np.tril(np.ones((c, c), np.float32))
    tri[1] = flip(tri[0])
    mask[0, 0] = mask[1, 0] = np.eye(c, dtype=np.float32)
    for li, m in enumerate(levels):
        expo = np.zeros((c, c), np.float32)
        lmask = np.zeros((c, c), np.float32)
        late = np.zeros((c,), np.float32)
        for r in range(c):
            blk, pos = divmod(r, 2 * m)
            ref = blk * 2 * m + m - 1
            if pos >= m:
                late[r] = 1.0
                expo[r, ref + 1:r + 1] = 1.0
                lmask[r, blk * 2 * m:blk * 2 * m + m] = 1.0
            else:
                expo[r, r + 1:ref + 1] = 1.0
        for d in range(2):
            ex = expo if d == 0 else flip(expo)
            qsel[d, li, :, 0] = late if d == 0 else late[::-1]
            mask[d, li + 1] = lmask if d == 0 else flip(lmask)
            if li < nb:
                for blk in range(c // (2 * m)):
                    ref = blk * 2 * m + m - 1
                    refs[d][li].append(ref if d == 0 else c - 1 - ref)
                if d == 1:
                    refs[d][li] = sorted(refs[d][li])
            else:
                tridiag = np.zeros_like(ex)
                for r in range(c):
                    for j, off in enumerate((-1, 0, 1)):
                        if 0 <= r + off < c:
                            coef[d, li - nb, j, r, 0] = ex[r, r + off]
                            tridiag[r, r + off] = ex[r, r + off]
                assert np.array_equal(tridiag, ex)
    lanes = lambda a: np.ascontiguousarray(np.broadcast_to(a, a.shape[:-1] + (HEAD_DIM,)))
    return tri, lanes(qsel), mask, lanes(coef), refs


def _scan_chunk(d, refs, q_ref, k_ref, v_ref, lf_ref, tri_ref, qsel_ref, mask_ref, coef_ref,
                s_ref, out_ref, row0):
    c = SCAN_CHUNK
    nb = SCAN_BCAST_LEVELS
    rows = pl.ds(row0, c)
    q_b = q_ref[rows, :]
    k_b = k_ref[rows, :]
    v_b = v_ref[rows, :]
    lf = lf_ref[rows, :]
    q = q_b.astype(F32)
    k = k_b.astype(F32)

    hi = lf.astype(BF16)
    r1 = lf - hi.astype(F32)
    mid = r1.astype(BF16)
    lo = (r1 - mid.astype(F32)).astype(BF16)
    g3 = _dot(tri_ref[d], jnp.concatenate([hi, mid, lo], axis=1))
    g = (g3[:, :HEAD_DIM] + g3[:, HEAD_DIM:2 * HEAD_DIM]) + g3[:, 2 * HEAD_DIM:]
    g_last = g[c - 1:c, :] if d == 0 else g[0:1, :]

    state = s_ref[...]
    qg = (q * jnp.exp(g)).astype(BF16)
    out = _dot(qg, state.astype(BF16))

    scores = mask_ref[d, 0] * _dot_nt(q_b, k_b)
    lf_dn = pltpu.roll(lf, 1, 0)
    lf_up = pltpu.roll(lf, c - 1, 0)
    for li, m in enumerate(SCAN_LEVELS):
        late = qsel_ref[d, li] > 0.5
        if li < nb:
            g_blk = jnp.concatenate(
                [jnp.broadcast_to(g[r:r + 1, :], (2 * m, HEAD_DIM)) for r in refs[d][li]], axis=0)
            expo = jnp.where(late, g - g_blk, g_blk - g)
        else:
            cf = coef_ref[d, li - nb]
            expo = cf[0] * lf_dn + cf[1] * lf + cf[2] * lf_up
        xx = (jnp.where(late, q, k) * jnp.exp(expo)).astype(BF16)
        scores = scores + mask_ref[d, li + 1] * _dot_nt(xx, xx)
    out = out + _dot(scores.astype(BF16), v_b)
    out_ref[rows, :] = out

    kg = (k * jnp.exp(g_last - g)).astype(BF16)
    decay = jnp.exp(jnp.broadcast_to(g_last, (HEAD_DIM, HEAD_DIM)).T)
    s_ref[...] = decay * state + _dot_tn(kg, v_b)


def _hgrn_scan_kernel(q_ref, v_ref, kf_ref, kb_ref, lff_ref, lfb_ref,
                      tri_ref, qsel_ref, mask_ref, coef_ref, o_ref,
                      s_scr, of_scr, ob_scr, *, hp, refs, norm_rows):
    seq = q_ref.shape[2]
    c = SCAN_CHUNK
    nc = seq // c
    s_scr[...] = jnp.zeros(s_scr.shape, F32)

    def body(ci, carry):
        row_f = pl.multiple_of(ci * c, c)
        row_b = pl.multiple_of((nc - 1 - ci) * c, c)
        for hd in range(hp):
            _scan_chunk(0, refs, q_ref.at[0, hd], kf_ref.at[0, hd], v_ref.at[0, hd], lff_ref.at[0, hd],
                        tri_ref, qsel_ref, mask_ref, coef_ref, s_scr.at[hd], of_scr.at[hd], row_f)
            _scan_chunk(1, refs, q_ref.at[0, hd], kb_ref.at[0, hd], v_ref.at[0, hd], lfb_ref.at[0, hd],
                        tri_ref, qsel_ref, mask_ref, coef_ref, s_scr.at[hp + hd], ob_scr.at[hd], row_b)
        return carry

    lax.fori_loop(0, nc, body, 0)

    def norm_body(j, carry):
        rows = pl.ds(pl.multiple_of(j * norm_rows, norm_rows), norm_rows)
        for hd in range(hp):
            o = of_scr[hd, rows, :] + ob_scr[hd, rows, :]
            ms = jnp.mean(o * o, axis=-1, keepdims=True)
            o_ref[0, hd, rows, :] = (o * lax.rsqrt(ms + EPS)).astype(o_ref.dtype)
        return carry

    lax.fori_loop(0, seq // norm_rows, norm_body, 0)


def _hgrn_scan(q, v, kf, kb, lff, lfb, *, hp=2, norm_rows=512):
    bsz, n_heads, seq, dh = q.shape
    assert dh == HEAD_DIM and n_heads % hp == 0 and seq % SCAN_CHUNK == 0 and seq % norm_rows == 0
    tri, qsel, mask, coef, refs = _scan_constants()
    consts = (jnp.asarray(tri, BF16), jnp.asarray(qsel), jnp.asarray(mask), jnp.asarray(coef))
    head_spec = pl.BlockSpec((1, hp, seq, dh), lambda b, h: (b, h, 0, 0))
    kern = functools.partial(_hgrn_scan_kernel, hp=hp, refs=refs, norm_rows=norm_rows)
    return pl.pallas_call(
        kern,
        name="hgrn_scan",
        grid=(bsz, n_heads // hp),
        in_specs=[head_spec] * 6 + [_const_spec(a.shape) for a in consts],
        out_specs=head_spec,
        out_shape=jax.ShapeDtypeStruct(q.shape, BF16),
        scratch_shapes=[
            pltpu.VMEM((2 * hp, dh, dh), F32),
            pltpu.VMEM((hp, seq, dh), F32),
            pltpu.VMEM((hp, seq, dh), F32),
        ],
        compiler_params=pltpu.CompilerParams(
            dimension_semantics=("parallel", "parallel"), vmem_limit_bytes=VMEM_LIMIT_BYTES),
    )(q, v, kf, kb, lff, lfb, *consts)


def _hgrn_out_kernel(x_ref, o_ref, z_ref, hnw_ref, wout_ref, fnw_ref, out_ref, *, final_norm):
    n_heads = o_ref.shape[1]
    o = jnp.concatenate([o_ref[0, hd] for hd in range(n_heads)], axis=-1).astype(F32)
    y = o * hnw_ref[...] * _silu(z_ref[0].astype(F32))
    res = x_ref[0] + _dot(y.astype(BF16), wout_ref[...])
    if final_norm:
        res = _rms_norm(res, fnw_ref[...])
    out_ref[0] = res


def _hgrn_out(x, o, z, head_norm_w, w_out, final_norm_w, *, final_norm, tt=512):
    bsz, seq, d = x.shape
    n_heads = o.shape[1]
    d_inner = z.shape[-1]
    assert seq % tt == 0
    kern = functools.partial(_hgrn_out_kernel, final_norm=final_norm)
    return pl.pallas_call(
        kern,
        name="hgrn_out",
        grid=(bsz, seq // tt),
        in_specs=[
            pl.BlockSpec((1, tt, d), lambda b, i: (b, i, 0)),
            pl.BlockSpec((1, n_heads, tt, HEAD_DIM), lambda b, i: (b, 0, i, 0)),
            pl.BlockSpec((1, tt, d_inner), lambda b, i: (b, i, 0)),
            _const_spec((1, d_inner)),
            _const_spec(w_out.shape),
            _const_spec((1, d)),
        ],
        out_specs=pl.BlockSpec((1, tt, d), lambda b, i: (b, i, 0)),
        out_shape=jax.ShapeDtypeStruct(x.shape, F32),
        compiler_params=pltpu.CompilerParams(
            dimension_semantics=("parallel", "parallel"), vmem_limit_bytes=VMEM_LIMIT_BYTES),
    )(x, o, z, head_norm_w.reshape(1, d_inner), w_out, final_norm_w.reshape(1, d))


def _hgrn_lower_bounds(lb_logits):
    p = jax.nn.softmax(lb_logits.astype(F32), axis=0)
    return jnp.cumsum(p, axis=0) - p[0]


def kernel(x, norm_w, final_norm_w, conv_w_in, conv_kernel, conv_w_out,
           hgrn_w_in, hgrn_lb_logits, hgrn_norm_w, hgrn_w_out):
    depth = norm_w.shape[0]
    lower_bounds = _hgrn_lower_bounds(hgrn_lb_logits)
    conv_w_in_b = conv_w_in.astype(BF16)
    conv_w_out_b = conv_w_out.astype(BF16)
    hgrn_w_in_b = hgrn_w_in.astype(BF16)
    hgrn_w_out_b = hgrn_w_out.astype(BF16)
    assert depth % 2 == 0
    for layer in range(depth):
        j = layer // 2
        if layer % 2 == 0:
            x = _conv_layer(x, norm_w[layer], conv_w_in_b[j], conv_kernel[j], conv_w_out_b[j])
        else:
            q, v, z, kf, kb, lff, lfb = _hgrn_in(x, norm_w[layer], hgrn_w_in_b[j], lower_bounds[j])
            o = _hgrn_scan(q, v, kf, kb, lff, lfb)
            x = _hgrn_out(x, o, z, hgrn_norm_w[j], hgrn_w_out_b[j], final_norm_w,
                          final_norm=(layer == depth - 1))
    return x
```

```python
import functools

import numpy as np
import jax
import jax.numpy as jnp
from jax import lax
from jax.experimental import pallas as pl
from jax.experimental.pallas import tpu as pltpu

HEAD_DIM = 128
EPS = 1e-6
LB_FLOOR = 1e-30
CONV_WIDTH = 3

V7X_LANES = 128
V7X_BF16_SUBLANES = 16
VMEM_LIMIT_BYTES = 56 * 1024 * 1024

SCAN_CHUNK = 64
SCAN_LEVELS = (32, 16, 8, 4, 2, 1)
SCAN_BCAST_LEVELS = 4
SCAN_FAST_RANGE = 60.0
SCAN_FAST_QMAX = 2.0 ** 20

F32 = jnp.float32
BF16 = jnp.bfloat16


def _dot(a, b):
    return jnp.dot(a, b, preferred_element_type=F32)


def _dot_nt(a, b):
    return lax.dot_general(a, b, (((1,), (1,)), ((), ())), preferred_element_type=F32)


def _dot_tn(a, b):
    return lax.dot_general(a, b, (((0,), (0,)), ((), ())), preferred_element_type=F32)


def _rms_norm(xv, w):
    ms = jnp.mean(xv * xv, axis=-1, keepdims=True)
    return xv * lax.rsqrt(ms + EPS) * w


def _silu(z):
    return z * jax.nn.sigmoid(z)


def _const_spec(shape):
    nd = len(shape)
    return pl.BlockSpec(shape, lambda *_: (0,) * nd, pipeline_mode=pl.Buffered(1))


def _conv_layer_kernel(x_ref, xp_ref, xn_ref, nw_ref, win_ref, ck_ref, wout_ref, o_ref, h_scr,
                       *, tt, te, halo):
    i = pl.program_id(1)
    nt = pl.num_programs(1)
    d_inner = wout_ref.shape[0]
    nw = nw_ref[...]
    x = x_ref[0]
    h_scr[halo:halo + tt, :] = _rms_norm(x, nw).astype(BF16)
    h_scr[0:halo, :] = jnp.where(i > 0, _rms_norm(xp_ref[0], nw), 0.0).astype(BF16)
    h_scr[halo + tt:, :] = jnp.where(i < nt - 1, _rms_norm(xn_ref[0], nw), 0.0).astype(BF16)

    rows = tt + 2 * halo
    acc = jnp.zeros((tt, o_ref.shape[-1]), F32)
    for e in range(d_inner // te):
        def cols(g):
            return slice(g * d_inner + e * te, g * d_inner + (e + 1) * te)
        h_main = h_scr[halo:halo + tt, :]
        h_ext = h_scr[...]
        b_gate = _dot(h_main, win_ref[:, cols(0)])
        v = _dot(h_ext, win_ref[:, cols(1)]) * _dot(h_ext, win_ref[:, cols(2)])
        z = _dot(h_main, win_ref[:, cols(3)])
        ck = ck_ref[:, e * te:(e + 1) * te]
        v_prev = pltpu.roll(v, 1, 0)[halo:halo + tt]
        v_next = pltpu.roll(v, rows - 1, 0)[halo:halo + tt]
        conv = ck[0:1] * v_prev + ck[1:2] * v[halo:halo + tt] + ck[2:3] * v_next
        y = b_gate * conv * _silu(z)
        acc = acc + _dot(y.astype(BF16), wout_ref[e * te:(e + 1) * te, :])
    o_ref[0] = x + acc


def _conv_layer(x, norm_w, w_in, conv_k, w_out, *, tt=512, te=512):
    bsz, seq, d = x.shape
    d_inner = w_out.shape[0]
    halo = V7X_BF16_SUBLANES
    assert seq % tt == 0 and tt % halo == 0 and d_inner % te == 0
    nt = seq // tt
    hb = tt // halo
    last_hb = seq // halo - 1
    kern = functools.partial(_conv_layer_kernel, tt=tt, te=te, halo=halo)
    return pl.pallas_call(
        kern,
        name="conv_layer",
        grid=(bsz, nt),
        in_specs=[
            pl.BlockSpec((1, tt, d), lambda b, i: (b, i, 0)),
            pl.BlockSpec((1, halo, d), lambda b, i: (b, jnp.maximum(i * hb - 1, 0), 0)),
            pl.BlockSpec((1, halo, d), lambda b, i: (b, jnp.minimum((i + 1) * hb, last_hb), 0)),
            _const_spec((1, d)),
            _const_spec(w_in.shape),
            _const_spec(conv_k.shape),
            _const_spec(w_out.shape),
        ],
        out_specs=pl.BlockSpec((1, tt, d), lambda b, i: (b, i, 0)),
        out_shape=jax.ShapeDtypeStruct(x.shape, F32),
        scratch_shapes=[pltpu.VMEM((tt + 2 * halo, d), BF16)],
        compiler_params=pltpu.CompilerParams(
            dimension_semantics=("parallel", "arbitrary"), vmem_limit_bytes=VMEM_LIMIT_BYTES),
    )(x, x, x, norm_w.reshape(1, d), w_in, conv_k, w_out)


def _hgrn_in_kernel(x_ref, nw_ref, win_ref, lbc_ref, lbf_ref,
                    q_ref, v_ref, z_ref, kf_ref, kb_ref, lff_ref, lfb_ref):
    d_inner = z_ref.shape[-1]
    n_heads = d_inner // HEAD_DIM
    h = _rms_norm(x_ref[0], nw_ref[...]).astype(BF16)

    def proj(g):
        return _dot(h, win_ref[:, g * d_inner:(g + 1) * d_inner])

    def put_heads(ref, val):
        for hd in range(n_heads):
            ref[0, hd] = val[:, hd * HEAD_DIM:(hd + 1) * HEAD_DIM].astype(ref.dtype)

    put_heads(q_ref, proj(0) * (HEAD_DIM ** -0.5))
    one_minus_lb = 1.0 - lbc_ref[...]
    lb_floor = lbf_ref[...]
    for g, k_ref, lf_ref in ((1, kf_ref, lff_ref), (2, kb_ref, lfb_ref)):
        fp = proj(g)
        e = jnp.exp(-jnp.abs(fp))
        r = 1.0 / (1.0 + e)
        er = e * r
        pos = fp >= 0.0
        sig = jnp.where(pos, r, er)
        nsig = jnp.where(pos, er, r)
        put_heads(lf_ref, jnp.log(lb_floor + one_minus_lb * sig))
        put_heads(k_ref, one_minus_lb * nsig)
    put_heads(v_ref, proj(3))
    z_ref[0] = proj(4).astype(z_ref.dtype)


def _hgrn_in(x, norm_w, w_in, lb, *, tt=256):
    bsz, seq, d = x.shape
    d_inner = lb.shape[-1]
    n_heads = d_inner // HEAD_DIM
    assert seq % tt == 0
    lbc = jnp.clip(lb, 0.0, 1.0 - 1e-6).reshape(1, d_inner)
    lbf = jnp.maximum(lbc, LB_FLOOR)
    head_shape = (bsz, n_heads, seq, HEAD_DIM)
    head_spec = pl.BlockSpec((1, n_heads, tt, HEAD_DIM), lambda b, i: (b, 0, i, 0))
    return pl.pallas_call(
        _hgrn_in_kernel,
        name="hgrn_in",
        grid=(bsz, seq // tt),
        in_specs=[
            pl.BlockSpec((1, tt, d), lambda b, i: (b, i, 0)),
            _const_spec((1, d)),
            _const_spec(w_in.shape),
            _const_spec((1, d_inner)),
            _const_spec((1, d_inner)),
        ],
        out_specs=[
            head_spec, head_spec,
            pl.BlockSpec((1, tt, d_inner), lambda b, i: (b, i, 0)),
            head_spec, head_spec, head_spec, head_spec,
        ],
        out_shape=[
            jax.ShapeDtypeStruct(head_shape, BF16),
            jax.ShapeDtypeStruct(head_shape, BF16),
            jax.ShapeDtypeStruct((bsz, seq, d_inner), BF16),
            jax.ShapeDtypeStruct(head_shape, BF16),
            jax.ShapeDtypeStruct(head_shape, BF16),
            jax.ShapeDtypeStruct(head_shape, F32),
            jax.ShapeDtypeStruct(head_shape, F32),
        ],
        compiler_params=pltpu.CompilerParams(
            dimension_semantics=("parallel", "parallel"), vmem_limit_bytes=VMEM_LIMIT_BYTES),
    )(x, norm_w.reshape(1, d), w_in, lbc, lbf)


def _scan_constants():
    c = SCAN_CHUNK
    levels = SCAN_LEVELS
    nb = SCAN_BCAST_LEVELS
    tri = np.zeros((2, c, c), np.float32)
    qsel = np.zeros((2, len(levels), c, 1), np.float32)
    mask = np.zeros((2, len(levels) + 1, c, c), np.float32)
    coef = np.zeros((2, len(levels) - nb, 3, c, 1), np.float32)
    refs = [[[] for _ in range(nb)] for _ in range(2)]
    flip = lambda a: a[::-1, ::-1]
    tri[0] = np.tril(np.ones((c, c), np.float32))
    tri[1] = flip(tri[0])
    mask[0, 0] = mask[1, 0] = np.eye(c, dtype=np.float32)
    for li, m in enumerate(levels):
        expo = np.zeros((c, c), np.float32)
        lmask = np.zeros((c, c), np.float32)
        late = np.zeros((c,), np.float32)
        for r in range(c):
            blk, pos = divmod(r, 2 * m)
            ref = blk * 2 * m + m - 1
            if pos >= m:
                late[r] = 1.0
                expo[r, ref + 1:r + 1] = 1.0
                lmask[r, blk * 2 * m:blk * 2 * m + m] = 1.0
            else:
                expo[r, r + 1:ref + 1] = 1.0
        for d in range(2):
            ex = expo if d == 0 else flip(expo)
            qsel[d, li, :, 0] = late if d == 0 else late[::-1]
            mask[d, li + 1] = lmask if d == 0 else flip(lmask)
            if li < nb:
                for blk in range(c // (2 * m)):
                    ref = blk * 2 * m + m - 1
                    refs[d][li].append(ref if d == 0 else c - 1 - ref)
                if d == 1:
                    refs[d][li] = sorted(refs[d][li])
            else:
                tridiag = np.zeros_like(ex)
                for r in range(c):
                    for j, off in enumerate((-1, 0, 1)):
                        if 0 <= r + off < c:
                            coef[d, li - nb, j, r, 0] = ex[r, r + off]
                            tridiag[r, r + off] = ex[r, r + off]
                assert np.array_equal(tridiag, ex)
    lanes = lambda a: np.ascontiguousarray(np.broadcast_to(a, a.shape[:-1] + (HEAD_DIM,)))
    return tri, lanes(qsel), mask, lanes(coef), refs


def _chunk_cumsum(d, lf, tri_ref):
    hi = lf.astype(BF16)
    r1 = lf - hi.astype(F32)
    mid = r1.astype(BF16)
    lo = (r1 - mid.astype(F32)).astype(BF16)
    g3 = _dot(tri_ref[d], jnp.concatenate([hi, mid, lo], axis=1))
    return (g3[:, :HEAD_DIM] + g3[:, HEAD_DIM:2 * HEAD_DIM]) + g3[:, 2 * HEAD_DIM:]


def _scan_pair_fast(chains, tri2_ref, causal2_ref, vsel_ref):
    c = SCAN_CHUNK
    half = c // 2
    dh = HEAD_DIM

    def per_chunk(fn):
        return jnp.concatenate([jnp.broadcast_to(fn(j), (c, dh)) for j in range(2)], axis=0)

    rows = [pl.ds(ch[7], 2 * c) for ch in chains]
    g2 = []
    for ch, rw in zip(chains, rows):
        lf = ch[4][rw, :]
        hi = lf.astype(BF16)
        mid = (lf - hi.astype(F32)).astype(BF16)
        g2.append(_dot(tri2_ref[ch[0]], jnp.concatenate([hi, mid], axis=1)))
    a_l, b_l, qg_l, kg_l, dcol_l = [], [], [], [], []
    for ch, rw, gg in zip(chains, rows, g2):
        d = ch[0]
        g = gg[:, :dh] + gg[:, dh:]
        mid_row = (lambda j: j * c + half - 1) if d == 0 else (lambda j: j * c + half)
        last_row = (lambda j: j * c + c - 1) if d == 0 else (lambda j: j * c)
        r_row = lambda j: g[mid_row(j):mid_row(j) + 1, :]
        gl_row = lambda j: g[last_row(j):last_row(j) + 1, :]
        dg = g - per_chunk(r_row)
        a = ch[1][rw, :].astype(F32) * jnp.exp(dg)
        b = ch[2][rw, :].astype(F32) * jnp.exp(-dg)
        a_l.append(a.astype(BF16))
        b_l.append(b.astype(BF16))
        qg_l.append((a * per_chunk(lambda j: jnp.exp(r_row(j)))).astype(BF16))
        kg_l.append((b * per_chunk(lambda j: jnp.exp(gl_row(j) - r_row(j)))).astype(BF16))
        dcol_l.append([jnp.exp(jnp.broadcast_to(gl_row(j), (dh, dh)).T) for j in range(2)])
    v_l = [ch[3][rw, :] for ch, rw in zip(chains, rows)]
    p_l = [_dot_nt(a, b) for a, b in zip(a_l, b_l)]
    u_l = [_dot_tn(kg, jnp.concatenate([v, v], axis=1) * vsel_ref[...]) for kg, v in zip(kg_l, v_l)]
    for ch, rw, p, u, qg, v, dcol in zip(chains, rows, p_l, u_l, qg_l, v_l, dcol_l):
        d = ch[0]
        sc = jnp.where(causal2_ref[d] > 0.5, p, 0.0).astype(BF16)
        s0 = ch[5][...]
        first, second = (0, 1) if d == 0 else (1, 0)
        s1 = dcol[first] * s0 + u[:, first * dh:(first + 1) * dh]
        ch[5][...] = dcol[second] * s1 + u[:, second * dh:(second + 1) * dh]
        s_by_chunk = (s0, s1) if d == 0 else (s1, s0)
        rhs = jnp.concatenate([
            jnp.concatenate([s_by_chunk[0].astype(BF16), s_by_chunk[1].astype(BF16)], axis=1),
            jnp.concatenate([v, v], axis=1)], axis=0)
        o2 = _dot(jnp.concatenate([qg, sc], axis=1), rhs)
        ch[6][rw, :] = jnp.concatenate([o2[:c, :dh], o2[c:, dh:]], axis=0)


def _scan_chunk(d, refs, q_ref, k_ref, v_ref, lf_ref, tri_ref, qsel_ref, mask_ref, coef_ref,
                s_ref, out_ref, row0):
    c = SCAN_CHUNK
    nb = SCAN_BCAST_LEVELS
    rows = pl.ds(row0, c)
    q_b = q_ref[rows, :]
    k_b = k_ref[rows, :]
    v_b = v_ref[rows, :]
    lf = lf_ref[rows, :]
    q = q_b.astype(F32)
    k = k_b.astype(F32)
    g = _chunk_cumsum(d, lf, tri_ref)
    g_last = g[c - 1:c, :] if d == 0 else g[0:1, :]

    state = s_ref[...]
    qg = (q * jnp.exp(g)).astype(BF16)
    out = _dot(qg, state.astype(BF16))

    scores = mask_ref[d, 0] * _dot_nt(q_b, k_b)
    lf_dn = pltpu.roll(lf, 1, 0)
    lf_up = pltpu.roll(lf, c - 1, 0)
    for li, m in enumerate(SCAN_LEVELS):
        late = qsel_ref[d, li] > 0.5
        if li < nb:
            g_blk = jnp.concatenate(
                [jnp.broadcast_to(g[r:r + 1, :], (2 * m, HEAD_DIM)) for r in refs[d][li]], axis=0)
            expo = jnp.where(late, g - g_blk, g_blk - g)
        else:
            cf = coef_ref[d, li - nb]
            expo = cf[0] * lf_dn + cf[1] * lf + cf[2] * lf_up
        xx = (jnp.where(late, q, k) * jnp.exp(expo)).astype(BF16)
        scores = scores + mask_ref[d, li + 1] * _dot_nt(xx, xx)
    out = out + _dot(scores.astype(BF16), v_b)
    out_ref[rows, :] = out

    kg = (k * jnp.exp(g_last - g)).astype(BF16)
    decay = jnp.exp(jnp.broadcast_to(g_last, (HEAD_DIM, HEAD_DIM)).T)
    s_ref[...] = decay * state + _dot_tn(kg, v_b)


def _hgrn_scan_kernel(q_ref, v_ref, kf_ref, kb_ref, lff_ref, lfb_ref,
                      tri_ref, tri2_ref, causal2_ref, vsel_ref, qsel_ref, mask_ref, coef_ref, o_ref,
                      s_scr, of_scr, ob_scr, *, hp, refs, norm_rows):
    seq = q_ref.shape[2]
    c = SCAN_CHUNK
    nc = seq // c
    s_scr[...] = jnp.zeros(s_scr.shape, F32)

    def chunk_rows(ci):
        return pl.multiple_of(ci * c, c), pl.multiple_of((nc - 1 - ci) * c, c)

    def fast_body(pi, carry):
        row_f = pl.multiple_of(pi * 2 * c, 2 * c)
        row_b = pl.multiple_of((nc // 2 - 1 - pi) * 2 * c, 2 * c)
        chains = []
        for hd in range(hp):
            chains.append((0, q_ref.at[0, hd], kf_ref.at[0, hd], v_ref.at[0, hd], lff_ref.at[0, hd],
                           s_scr.at[hd], of_scr.at[hd], row_f))
            chains.append((1, q_ref.at[0, hd], kb_ref.at[0, hd], v_ref.at[0, hd], lfb_ref.at[0, hd],
                           s_scr.at[hp + hd], ob_scr.at[hd], row_b))
        _scan_pair_fast(chains, tri2_ref, causal2_ref, vsel_ref)
        return carry

    def robust_body(ci, carry):
        row_f, row_b = chunk_rows(ci)
        for hd in range(hp):
            _scan_chunk(0, refs, q_ref.at[0, hd], kf_ref.at[0, hd], v_ref.at[0, hd], lff_ref.at[0, hd],
                        tri_ref, qsel_ref, mask_ref, coef_ref, s_scr.at[hd], of_scr.at[hd], row_f)
            _scan_chunk(1, refs, q_ref.at[0, hd], kb_ref.at[0, hd], v_ref.at[0, hd], lfb_ref.at[0, hd],
                        tri_ref, qsel_ref, mask_ref, coef_ref, s_scr.at[hp + hd], ob_scr.at[hd], row_b)
        return carry

    half = c // 2
    min_half = jnp.float32(0.0)
    max_q = jnp.float32(0.0)
    for hd in range(hp):
        for lf_ref in (lff_ref, lfb_ref):
            sums = jnp.sum(lf_ref[0, hd].reshape(seq // half, half, HEAD_DIM), axis=1)
            min_half = jnp.minimum(min_half, jnp.min(sums))
        max_q = jnp.maximum(max_q, jnp.max(jnp.abs(q_ref[0, hd].astype(F32))))
    fast_ok = jnp.logical_and(min_half >= -SCAN_FAST_RANGE, max_q <= SCAN_FAST_QMAX)

    @pl.when(fast_ok)
    def _():
        lax.fori_loop(0, nc // 2, fast_body, 0)

    @pl.when(jnp.logical_not(fast_ok))
    def _():
        lax.fori_loop(0, nc, robust_body, 0)

    def norm_body(j, carry):
        rows = pl.ds(pl.multiple_of(j * norm_rows, norm_rows), norm_rows)
        for hd in range(hp):
            o = of_scr[hd, rows, :] + ob_scr[hd, rows, :]
            ms = jnp.mean(o * o, axis=-1, keepdims=True)
            o_ref[0, hd, rows, :] = (o * lax.rsqrt(ms + EPS)).astype(o_ref.dtype)
        return carry

    lax.fori_loop(0, seq // norm_rows, norm_body, 0)


def _hgrn_scan(q, v, kf, kb, lff, lfb, *, hp=2, norm_rows=512):
    bsz, n_heads, seq, dh = q.shape
    assert dh == HEAD_DIM and n_heads % hp == 0 and seq % (2 * SCAN_CHUNK) == 0 and seq % norm_rows == 0
    tri, qsel, mask, coef, refs = _scan_constants()
    c = SCAN_CHUNK
    tri2 = np.zeros((2, 2 * c, 2 * c), np.float32)
    tri2[:, :c, :c] = tri
    tri2[:, c:, c:] = tri
    vsel = np.zeros((2 * c, 2 * dh), np.float32)
    vsel[:c, :dh] = 1.0
    vsel[c:, dh:] = 1.0
    consts = (jnp.asarray(tri, BF16), jnp.asarray(tri2, BF16), jnp.asarray(tri2), jnp.asarray(vsel, BF16),
              jnp.asarray(qsel), jnp.asarray(mask), jnp.asarray(coef))
    head_spec = pl.BlockSpec((1, hp, seq, dh), lambda b, h: (b, h, 0, 0))
    kern = functools.partial(_hgrn_scan_kernel, hp=hp, refs=refs, norm_rows=norm_rows)
    return pl.pallas_call(
        kern,
        name="hgrn_scan",
        grid=(bsz, n_heads // hp),
        in_specs=[head_spec] * 6 + [_const_spec(a.shape) for a in consts],
        out_specs=head_spec,
        out_shape=jax.ShapeDtypeStruct(q.shape, BF16),
        scratch_shapes=[
            pltpu.VMEM((2 * hp, dh, dh), F32),
            pltpu.VMEM((hp, seq, dh), F32),
            pltpu.VMEM((hp, seq, dh), F32),
        ],
        compiler_params=pltpu.CompilerParams(
            dimension_semantics=("parallel", "parallel"), vmem_limit_bytes=VMEM_LIMIT_BYTES),
    )(q, v, kf, kb, lff, lfb, *consts)


def _hgrn_out_kernel(x_ref, o_ref, z_ref, hnw_ref, wout_ref, fnw_ref, out_ref, *, final_norm):
    n_heads = o_ref.shape[1]
    o = jnp.concatenate([o_ref[0, hd] for hd in range(n_heads)], axis=-1).astype(F32)
    y = o * hnw_ref[...] * _silu(z_ref[0].astype(F32))
    res = x_ref[0] + _dot(y.astype(BF16), wout_ref[...])
    if final_norm:
        res = _rms_norm(res, fnw_ref[...])
    out_ref[0] = res


def _hgrn_out(x, o, z, head_norm_w, w_out, final_norm_w, *, final_norm, tt=512):
    bsz, seq, d = x.shape
    n_heads = o.shape[1]
    d_inner = z.shape[-1]
    assert seq % tt == 0
    kern = functools.partial(_hgrn_out_kernel, final_norm=final_norm)
    return pl.pallas_call(
        kern,
        name="hgrn_out",
        grid=(bsz, seq // tt),
        in_specs=[
            pl.BlockSpec((1, tt, d), lambda b, i: (b, i, 0)),
            pl.BlockSpec((1, n_heads, tt, HEAD_DIM), lambda b, i: (b, 0, i, 0)),
            pl.BlockSpec((1, tt, d_inner), lambda b, i: (b, i, 0)),
            _const_spec((1, d_inner)),
            _const_spec(w_out.shape),
            _const_spec((1, d)),
        ],
        out_specs=pl.BlockSpec((1, tt, d), lambda b, i: (b, i, 0)),
        out_shape=jax.ShapeDtypeStruct(x.shape, F32),
        compiler_params=pltpu.CompilerParams(
            dimension_semantics=("parallel", "parallel"), vmem_limit_bytes=VMEM_LIMIT_BYTES),
    )(x, o, z, head_norm_w.reshape(1, d_inner), w_out, final_norm_w.reshape(1, d))


def _hgrn_lower_bounds(lb_logits):
    p = jax.nn.softmax(lb_logits.astype(F32), axis=0)
    return jnp.cumsum(p, axis=0) - p[0]


def kernel(x, norm_w, final_norm_w, conv_w_in, conv_kernel, conv_w_out,
           hgrn_w_in, hgrn_lb_logits, hgrn_norm_w, hgrn_w_out):
    depth = norm_w.shape[0]
    lower_bounds = _hgrn_lower_bounds(hgrn_lb_logits)
    conv_w_in_b = conv_w_in.astype(BF16)
    conv_w_out_b = conv_w_out.astype(BF16)
    hgrn_w_in_b = hgrn_w_in.astype(BF16)
    hgrn_w_out_b = hgrn_w_out.astype(BF16)
    assert depth % 2 == 0
    for layer in range(depth):
        j = layer // 2
        if layer % 2 == 0:
            x = _conv_layer(x, norm_w[layer], conv_w_in_b[j], conv_kernel[j], conv_w_out_b[j])
        else:
            q, v, z, kf, kb, lff, lfb = _hgrn_in(x, norm_w[layer], hgrn_w_in_b[j], lower_bounds[j])
            o = _hgrn_scan(q, v, kf, kb, lff, lfb)
            x = _hgrn_out(x, o, z, hgrn_norm_w[j], hgrn_w_out_b[j], final_norm_w,
                          final_norm=(layer == depth - 1))
    return x
```

```python
import functools

import numpy as np
import jax
import jax.numpy as jnp
from jax import lax
from jax.experimental import pallas as pl
from jax.experimental.pallas import tpu as pltpu

HEAD_DIM = 128
EPS = 1e-6
LB_FLOOR = 1e-30
CONV_WIDTH = 3

V7X_LANES = 128
V7X_BF16_SUBLANES = 16
VMEM_LIMIT_BYTES = 56 * 1024 * 1024

SCAN_CHUNK = 64
SCAN_LEVELS = (32, 16, 8, 4, 2, 1)
SCAN_BCAST_LEVELS = 4
SCAN_FAST_RANGE = 86.0
SCAN_FAST_QMAX = 2.0 ** 20
SCAN_FAST_UNROLL = 4
LOG2_E = 1.4426950408889634

F32 = jnp.float32
BF16 = jnp.bfloat16


def _dot(a, b):
    return jnp.dot(a, b, preferred_element_type=F32)


def _dot_nt(a, b):
    return lax.dot_general(a, b, (((1,), (1,)), ((), ())), preferred_element_type=F32)


def _dot_tn(a, b):
    return lax.dot_general(a, b, (((0,), (0,)), ((), ())), preferred_element_type=F32)


def _rms_norm(xv, w):
    ms = jnp.mean(xv * xv, axis=-1, keepdims=True)
    return xv * lax.rsqrt(ms + EPS) * w


def _silu(z):
    return z * jax.nn.sigmoid(z)


def _const_spec(shape):
    nd = len(shape)
    return pl.BlockSpec(shape, lambda *_: (0,) * nd, pipeline_mode=pl.Buffered(1))


def _conv_layer_kernel(x_ref, xp_ref, xn_ref, nw_ref, win_ref, ck_ref, wout_ref, o_ref, h_scr,
                       *, tt, te, halo):
    i = pl.program_id(1)
    nt = pl.num_programs(1)
    d_inner = wout_ref.shape[0]
    nw = nw_ref[...]
    x = x_ref[0]
    h_scr[halo:halo + tt, :] = _rms_norm(x, nw).astype(BF16)
    h_scr[0:halo, :] = jnp.where(i > 0, _rms_norm(xp_ref[0], nw), 0.0).astype(BF16)
    h_scr[halo + tt:, :] = jnp.where(i < nt - 1, _rms_norm(xn_ref[0], nw), 0.0).astype(BF16)

    rows = tt + 2 * halo
    acc = jnp.zeros((tt, o_ref.shape[-1]), F32)
    for e in range(d_inner // te):
        def cols(g):
            return slice(g * d_inner + e * te, g * d_inner + (e + 1) * te)
        h_main = h_scr[halo:halo + tt, :]
        h_ext = h_scr[...]
        b_gate = _dot(h_main, win_ref[:, cols(0)])
        v = _dot(h_ext, win_ref[:, cols(1)]) * _dot(h_ext, win_ref[:, cols(2)])
        z = _dot(h_main, win_ref[:, cols(3)])
        ck = ck_ref[:, e * te:(e + 1) * te]
        v_prev = pltpu.roll(v, 1, 0)[halo:halo + tt]
        v_next = pltpu.roll(v, rows - 1, 0)[halo:halo + tt]
        conv = ck[0:1] * v_prev + ck[1:2] * v[halo:halo + tt] + ck[2:3] * v_next
        y = b_gate * conv * _silu(z)
        acc = acc + _dot(y.astype(BF16), wout_ref[e * te:(e + 1) * te, :])
    o_ref[0] = x + acc


def _conv_layer(x, norm_w, w_in, conv_k, w_out, *, tt=512, te=512):
    bsz, seq, d = x.shape
    d_inner = w_out.shape[0]
    halo = V7X_BF16_SUBLANES
    assert seq % tt == 0 and tt % halo == 0 and d_inner % te == 0
    nt = seq // tt
    hb = tt // halo
    last_hb = seq // halo - 1
    kern = functools.partial(_conv_layer_kernel, tt=tt, te=te, halo=halo)
    return pl.pallas_call(
        kern,
        name="conv_layer",
        grid=(bsz, nt),
        in_specs=[
            pl.BlockSpec((1, tt, d), lambda b, i: (b, i, 0)),
            pl.BlockSpec((1, halo, d), lambda b, i: (b, jnp.maximum(i * hb - 1, 0), 0)),
            pl.BlockSpec((1, halo, d), lambda b, i: (b, jnp.minimum((i + 1) * hb, last_hb), 0)),
            _const_spec((1, d)),
            _const_spec(w_in.shape),
            _const_spec(conv_k.shape),
            _const_spec(w_out.shape),
        ],
        out_specs=pl.BlockSpec((1, tt, d), lambda b, i: (b, i, 0)),
        out_shape=jax.ShapeDtypeStruct(x.shape, F32),
        scratch_shapes=[pltpu.VMEM((tt + 2 * halo, d), BF16)],
        compiler_params=pltpu.CompilerParams(
            dimension_semantics=("parallel", "arbitrary"), vmem_limit_bytes=VMEM_LIMIT_BYTES),
    )(x, x, x, norm_w.reshape(1, d), w_in, conv_k, w_out)


def _hgrn_in_kernel(x_ref, nw_ref, win_ref, lbc_ref, lbf_ref, q_ref, v_ref, lff_ref, lfb_ref, *, cw):
    d_inner = lbc_ref.shape[-1]
    h = _rms_norm(x_ref[0], nw_ref[...]).astype(BF16)
    hpc = cw // HEAD_DIM

    def proj(g, j):
        return _dot(h, win_ref[:, g * d_inner + j * cw:g * d_inner + (j + 1) * cw])

    def put_heads(ref, j, val):
        for i in range(hpc):
            ref[0, j * hpc + i] = val[:, i * HEAD_DIM:(i + 1) * HEAD_DIM].astype(ref.dtype)

    def log2_forget(fp, j):
        one_minus_lb = 1.0 - lbc_ref[:, j * cw:(j + 1) * cw]
        lb_floor = lbf_ref[:, j * cw:(j + 1) * cw]
        e = jnp.exp(-jnp.abs(fp))
        r = 1.0 / (1.0 + e)
        sig = jnp.where(fp >= 0.0, r, e * r)
        return jnp.log(lb_floor + one_minus_lb * sig) * LOG2_E

    for j in range(d_inner // cw):
        fp_fw = proj(1, j)
        put_heads(q_ref, j, proj(0, j) * (HEAD_DIM ** -0.5))
        put_heads(lff_ref, j, log2_forget(fp_fw, j))
        fp_bw = proj(2, j)
        put_heads(v_ref, j, proj(3, j))
        put_heads(lfb_ref, j, log2_forget(fp_bw, j))


def _hgrn_in(x, norm_w, w_qffi, lbc, lbf, *, tt=512, cw=512):
    bsz, seq, d = x.shape
    d_inner = lbc.shape[-1]
    n_heads = d_inner // HEAD_DIM
    assert seq % tt == 0 and d_inner % cw == 0 and cw % HEAD_DIM == 0 and w_qffi.shape == (d, 4 * d_inner)
    head_shape = (bsz, n_heads, seq, HEAD_DIM)
    head_spec = pl.BlockSpec((1, n_heads, tt, HEAD_DIM), lambda b, i: (b, 0, i, 0))
    return pl.pallas_call(
        functools.partial(_hgrn_in_kernel, cw=cw),
        name="hgrn_in",
        grid=(bsz, seq // tt),
        in_specs=[
            pl.BlockSpec((1, tt, d), lambda b, i: (b, i, 0)),
            _const_spec((1, d)),
            _const_spec(w_qffi.shape),
            _const_spec((1, d_inner)),
            _const_spec((1, d_inner)),
        ],
        out_specs=[head_spec] * 4,
        out_shape=[
            jax.ShapeDtypeStruct(head_shape, BF16),
            jax.ShapeDtypeStruct(head_shape, BF16),
            jax.ShapeDtypeStruct(head_shape, F32),
            jax.ShapeDtypeStruct(head_shape, F32),
        ],
        compiler_params=pltpu.CompilerParams(
            dimension_semantics=("parallel", "parallel"), vmem_limit_bytes=VMEM_LIMIT_BYTES),
    )(x, norm_w.reshape(1, d), w_qffi, lbc, lbf)


def _scan_constants():
    c = SCAN_CHUNK
    levels = SCAN_LEVELS
    nb = SCAN_BCAST_LEVELS
    tri = np.zeros((2, c, c), np.float32)
    qsel = np.zeros((2, len(levels), c, 1), np.float32)
    mask = np.zeros((2, len(levels) + 1, c, c), np.float32)
    coef = np.zeros((2, len(levels) - nb, 3, c, 1), np.float32)
    refs = [[[] for _ in range(nb)] for _ in range(2)]
    flip = lambda a: a[::-1, ::-1]
    tri[0] = np.tril(np.ones((c, c), np.float32))
    tri[1] = flip(tri[0])
    mask[0, 0] = mask[1, 0] = np.eye(c, dtype=np.float32)
    for li, m in enumerate(levels):
        expo = np.zeros((c, c), np.float32)
        lmask = np.zeros((c, c), np.float32)
        late = np.zeros((c,), np.float32)
        for r in range(c):
            blk, pos = divmod(r, 2 * m)
            ref = blk * 2 * m + m - 1
            if pos >= m:
                late[r] = 1.0
                expo[r, ref + 1:r + 1] = 1.0
                lmask[r, blk * 2 * m:blk * 2 * m + m] = 1.0
            else:
                expo[r, r + 1:ref + 1] = 1.0
        for d in range(2):
            ex = expo if d == 0 else flip(expo)
            qsel[d, li, :, 0] = late if d == 0 else late[::-1]
            mask[d, li + 1] = lmask if d == 0 else flip(lmask)
            if li < nb:
                for blk in range(c // (2 * m)):
                    ref = blk * 2 * m + m - 1
                    refs[d][li].append(ref if d == 0 else c - 1 - ref)
                if d == 1:
                    refs[d][li] = sorted(refs[d][li])
            else:
                tridiag = np.zeros_like(ex)
                for r in range(c):
                    for j, off in enumerate((-1, 0, 1)):
                        if 0 <= r + off < c:
                            coef[d, li - nb, j, r, 0] = ex[r, r + off]
                            tridiag[r, r + off] = ex[r, r + off]
                assert np.array_equal(tridiag, ex)
    lanes = lambda a: np.ascontiguousarray(np.broadcast_to(a, a.shape[:-1] + (HEAD_DIM,)))
    return tri, lanes(qsel), mask, lanes(coef), refs


def _chunk_cumsum(d, lf, tri_ref):
    hi = lf.astype(BF16)
    r1 = lf - hi.astype(F32)
    mid = r1.astype(BF16)
    lo = (r1 - mid.astype(F32)).astype(BF16)
    g3 = _dot(tri_ref[d], jnp.concatenate([hi, mid, lo], axis=1))
    return (g3[:, :HEAD_DIM] + g3[:, HEAD_DIM:2 * HEAD_DIM]) + g3[:, 2 * HEAD_DIM:]


def _key_from_log2f(lf, kadd):
    return (1.0 - jnp.exp2(lf)) + kadd


def _emit(acc_ref, o_ref, rows, out, final):
    if final:
        o_ref[rows, :] = (acc_ref[rows, :] + out).astype(o_ref.dtype)
    else:
        acc_ref[rows, :] = out


def _scan_pair_fast(chains, tri2_ref, causal2_ref, vsel_ref, final):
    c = SCAN_CHUNK
    half = c // 2
    dh = HEAD_DIM

    def per_chunk(fn):
        return jnp.concatenate([jnp.broadcast_to(fn(j), (c, dh)) for j in range(2)], axis=0)

    rows = [pl.ds(ch[8], 2 * c) for ch in chains]
    lf_l = [ch[4][rw, :] for ch, rw in zip(chains, rows)]
    g2 = []
    for ch, lf in zip(chains, lf_l):
        hi = lf.astype(BF16)
        mid = (lf - hi.astype(F32)).astype(BF16)
        g2.append(_dot(tri2_ref[ch[0]], jnp.concatenate([hi, mid], axis=1)))
    a_l, b_l, qg_l, kg_l, dcol_l = [], [], [], [], []
    for ch, rw, lf, gg in zip(chains, rows, lf_l, g2):
        d = ch[0]
        g = gg[:, :dh] + gg[:, dh:]
        mid_row = (lambda j: j * c + half - 1) if d == 0 else (lambda j: j * c + half)
        last_row = (lambda j: j * c + c - 1) if d == 0 else (lambda j: j * c)
        r_row = lambda j: g[mid_row(j):mid_row(j) + 1, :]
        gl_row = lambda j: g[last_row(j):last_row(j) + 1, :]
        dg = g - per_chunk(r_row)
        a = ch[1][rw, :] * jnp.exp2(dg).astype(BF16)
        b = _key_from_log2f(lf, ch[2]).astype(BF16) * jnp.exp2(-dg).astype(BF16)
        a_l.append(a)
        b_l.append(b)
        qg_l.append(a * per_chunk(lambda j: jnp.exp2(r_row(j))).astype(BF16))
        kg_l.append(b * per_chunk(lambda j: jnp.exp2(gl_row(j) - r_row(j))).astype(BF16))
        dcol_l.append([jnp.broadcast_to(jnp.exp2(gl_row(j)), (dh, dh)).T for j in range(2)])
    v_l = [ch[3][rw, :] for ch, rw in zip(chains, rows)]
    p_l = [_dot_nt(a, b) for a, b in zip(a_l, b_l)]
    u_l = [_dot_tn(kg, jnp.concatenate([v, v], axis=1) * vsel_ref[...]) for kg, v in zip(kg_l, v_l)]
    for ch, rw, p, u, qg, v, dcol in zip(chains, rows, p_l, u_l, qg_l, v_l, dcol_l):
        d = ch[0]
        sc = jnp.where(causal2_ref[d] > 0.5, p, 0.0).astype(BF16)
        s0 = ch[5][...]
        first, second = (0, 1) if d == 0 else (1, 0)
        s1 = dcol[first] * s0 + u[:, first * dh:(first + 1) * dh]
        ch[5][...] = dcol[second] * s1 + u[:, second * dh:(second + 1) * dh]
        s_by_chunk = (s0, s1) if d == 0 else (s1, s0)
        rhs = jnp.concatenate([
            jnp.concatenate([s_by_chunk[0].astype(BF16), s_by_chunk[1].astype(BF16)], axis=1),
            jnp.concatenate([v, v], axis=1)], axis=0)
        o2 = _dot(jnp.concatenate([qg, sc], axis=1), rhs)
        _emit(ch[6], ch[7], rw, jnp.concatenate([o2[:c, :dh], o2[c:, dh:]], axis=0), final)


def _scan_chunk(d, refs, q_ref, kadd, v_ref, lf_ref, tri_ref, qsel_ref, mask_ref, coef_ref,
                s_ref, acc_ref, o_ref, row0, final):
    c = SCAN_CHUNK
    nb = SCAN_BCAST_LEVELS
    rows = pl.ds(row0, c)
    q_b = q_ref[rows, :]
    v_b = v_ref[rows, :]
    lf = lf_ref[rows, :]
    q = q_b.astype(F32)
    k = _key_from_log2f(lf, kadd)
    g = _chunk_cumsum(d, lf, tri_ref)
    g_last = g[c - 1:c, :] if d == 0 else g[0:1, :]

    state = s_ref[...]
    qg = (q * jnp.exp2(g)).astype(BF16)
    out = _dot(qg, state.astype(BF16))

    scores = mask_ref[d, 0] * _dot_nt(q_b, k.astype(BF16))
    lf_dn = pltpu.roll(lf, 1, 0)
    lf_up = pltpu.roll(lf, c - 1, 0)
    for li, m in enumerate(SCAN_LEVELS):
        late = qsel_ref[d, li] > 0.5
        if li < nb:
            g_blk = jnp.concatenate(
                [jnp.broadcast_to(g[r:r + 1, :], (2 * m, HEAD_DIM)) for r in refs[d][li]], axis=0)
            expo = jnp.where(late, g - g_blk, g_blk - g)
        else:
            cf = coef_ref[d, li - nb]
            expo = cf[0] * lf_dn + cf[1] * lf + cf[2] * lf_up
        xx = (jnp.where(late, q, k) * jnp.exp2(expo)).astype(BF16)
        scores = scores + mask_ref[d, li + 1] * _dot_nt(xx, xx)
    out = out + _dot(scores.astype(BF16), v_b)
    _emit(acc_ref, o_ref, rows, out, final)

    kg = (k * jnp.exp2(g_last - g)).astype(BF16)
    decay = jnp.broadcast_to(jnp.exp2(g_last), (HEAD_DIM, HEAD_DIM)).T
    s_ref[...] = decay * state + _dot_tn(kg, v_b)


def _hgrn_scan_kernel(q_ref, v_ref, lff_ref, lfb_ref, kadd_ref,
                      tri_ref, tri2_ref, causal2_ref, vsel_ref, qsel_ref, mask_ref, coef_ref, o_ref,
                      s_scr, acc_scr, *, hp, refs):
    seq = q_ref.shape[2]
    c = SCAN_CHUNK
    nc = seq // c
    n_pairs = nc // 2
    n_trips = n_pairs // SCAN_FAST_UNROLL
    s_scr[...] = jnp.zeros(s_scr.shape, F32)

    def fast_body(final, ti, carry):
        chains = []
        for j in range(SCAN_FAST_UNROLL):
            pi = ti * SCAN_FAST_UNROLL + j
            row_f = pl.multiple_of(pi * 2 * c, 2 * c)
            row_b = pl.multiple_of((n_pairs - 1 - pi) * 2 * c, 2 * c)
            for hd in range(hp):
                chains.append((0, q_ref.at[0, hd], kadd_ref[hd], v_ref.at[0, hd], lff_ref.at[0, hd],
                               s_scr.at[hd], acc_scr.at[hd], o_ref.at[0, hd], row_f))
                chains.append((1, q_ref.at[0, hd], kadd_ref[hd], v_ref.at[0, hd], lfb_ref.at[0, hd],
                               s_scr.at[hp + hd], acc_scr.at[hd], o_ref.at[0, hd], row_b))
        _scan_pair_fast(chains, tri2_ref, causal2_ref, vsel_ref, final)
        return carry

    def robust_body(final, ci, carry):
        row_f = pl.multiple_of(ci * c, c)
        row_b = pl.multiple_of((nc - 1 - ci) * c, c)
        for hd in range(hp):
            _scan_chunk(0, refs, q_ref.at[0, hd], kadd_ref[hd], v_ref.at[0, hd], lff_ref.at[0, hd],
                        tri_ref, qsel_ref, mask_ref, coef_ref,
                        s_scr.at[hd], acc_scr.at[hd], o_ref.at[0, hd], row_f, final)
            _scan_chunk(1, refs, q_ref.at[0, hd], kadd_ref[hd], v_ref.at[0, hd], lfb_ref.at[0, hd],
                        tri_ref, qsel_ref, mask_ref, coef_ref,
                        s_scr.at[hp + hd], acc_scr.at[hd], o_ref.at[0, hd], row_b, final)
        return carry

    half = c // 2
    min_half = jnp.float32(0.0)
    max_q = jnp.float32(0.0)
    for hd in range(hp):
        for lf_ref in (lff_ref, lfb_ref):
            sums = jnp.sum(lf_ref[0, hd].reshape(seq // half, half, HEAD_DIM), axis=1)
            min_half = jnp.minimum(min_half, jnp.min(sums))
        max_q = jnp.maximum(max_q, jnp.max(jnp.abs(q_ref[0, hd].astype(F32))))
    fast_ok = jnp.logical_and(min_half >= -SCAN_FAST_RANGE, max_q <= SCAN_FAST_QMAX)

    @pl.when(fast_ok)
    def _():
        lax.fori_loop(0, n_trips // 2, functools.partial(fast_body, False), 0)
        lax.fori_loop(n_trips // 2, n_trips, functools.partial(fast_body, True), 0)

    @pl.when(jnp.logical_not(fast_ok))
    def _():
        lax.fori_loop(0, nc // 2, functools.partial(robust_body, False), 0)
        lax.fori_loop(nc // 2, nc, functools.partial(robust_body, True), 0)


def _hgrn_scan(q, v, lff, lfb, kadd, *, hp=2):
    bsz, n_heads, seq, dh = q.shape
    assert dh == HEAD_DIM and n_heads % hp == 0
    assert seq % (4 * SCAN_CHUNK * SCAN_FAST_UNROLL) == 0
    tri, qsel, mask, coef, refs = _scan_constants()
    c = SCAN_CHUNK
    tri2 = np.zeros((2, 2 * c, 2 * c), np.float32)
    tri2[:, :c, :c] = tri
    tri2[:, c:, c:] = tri
    vsel = np.zeros((2 * c, 2 * dh), np.float32)
    vsel[:c, :dh] = 1.0
    vsel[c:, dh:] = 1.0
    consts = (jnp.asarray(tri, BF16), jnp.asarray(tri2, BF16), jnp.asarray(tri2), jnp.asarray(vsel, BF16),
              jnp.asarray(qsel), jnp.asarray(mask), jnp.asarray(coef))
    head_spec = pl.BlockSpec((1, hp, seq, dh), lambda b, h: (b, h, 0, 0))
    kern = functools.partial(_hgrn_scan_kernel, hp=hp, refs=refs)
    return pl.pallas_call(
        kern,
        name="hgrn_scan",
        grid=(bsz, n_heads // hp),
        in_specs=[head_spec] * 4 + [pl.BlockSpec((hp, 1, dh), lambda b, h: (h, 0, 0))]
        + [_const_spec(a.shape) for a in consts],
        out_specs=head_spec,
        out_shape=jax.ShapeDtypeStruct(q.shape, BF16),
        scratch_shapes=[
            pltpu.VMEM((2 * hp, dh, dh), F32),
            pltpu.VMEM((hp, seq, dh), F32),
        ],
        compiler_params=pltpu.CompilerParams(
            dimension_semantics=("parallel", "parallel"), vmem_limit_bytes=VMEM_LIMIT_BYTES),
    )(q, v, lff, lfb, kadd, *consts)


def _hgrn_out_kernel(x_ref, o_ref, nw_ref, wz_ref, hnw_ref, wout_ref, fnw_ref, out_ref, *, final_norm):
    n_heads = o_ref.shape[1]
    x = x_ref[0]
    z = _dot(_rms_norm(x, nw_ref[...]).astype(BF16), wz_ref[...])
    heads = []
    for hd in range(n_heads):
        o = o_ref[0, hd].astype(F32)
        heads.append(o * lax.rsqrt(jnp.mean(o * o, axis=-1, keepdims=True) + EPS))
    y = jnp.concatenate(heads, axis=-1) * hnw_ref[...] * _silu(z)
    res = x + _dot(y.astype(BF16), wout_ref[...])
    if final_norm:
        res = _rms_norm(res, fnw_ref[...])
    out_ref[0] = res


def _hgrn_out(x, o, norm_w, w_z, head_norm_w, w_out, final_norm_w, *, final_norm, tt=512):
    bsz, seq, d = x.shape
    n_heads = o.shape[1]
    d_inner = w_out.shape[0]
    assert seq % tt == 0 and w_z.shape == (d, d_inner)
    kern = functools.partial(_hgrn_out_kernel, final_norm=final_norm)
    return pl.pallas_call(
        kern,
        name="hgrn_out",
        grid=(bsz, seq // tt),
        in_specs=[
            pl.BlockSpec((1, tt, d), lambda b, i: (b, i, 0)),
            pl.BlockSpec((1, n_heads, tt, HEAD_DIM), lambda b, i: (b, 0, i, 0)),
            _const_spec((1, d)),
            _const_spec(w_z.shape),
            _const_spec((1, d_inner)),
            _const_spec(w_out.shape),
            _const_spec((1, d)),
        ],
        out_specs=pl.BlockSpec((1, tt, d), lambda b, i: (b, i, 0)),
        out_shape=jax.ShapeDtypeStruct(x.shape, F32),
        compiler_params=pltpu.CompilerParams(
            dimension_semantics=("parallel", "parallel"), vmem_limit_bytes=VMEM_LIMIT_BYTES),
    )(x, o, norm_w.reshape(1, d), w_z, head_norm_w.reshape(1, d_inner), w_out, final_norm_w.reshape(1, d))


def _hgrn_lower_bounds(lb_logits):
    p = jax.nn.softmax(lb_logits.astype(F32), axis=0)
    return jnp.cumsum(p, axis=0) - p[0]


def kernel(x, norm_w, final_norm_w, conv_w_in, conv_kernel, conv_w_out,
           hgrn_w_in, hgrn_lb_logits, hgrn_norm_w, hgrn_w_out):
    depth = norm_w.shape[0]
    d_inner = hgrn_w_out.shape[1]
    n_heads = d_inner // HEAD_DIM
    lower_bounds = _hgrn_lower_bounds(hgrn_lb_logits)
    conv_w_in_b = conv_w_in.astype(BF16)
    conv_w_out_b = conv_w_out.astype(BF16)
    hgrn_w_qffi_b = hgrn_w_in[:, :, :4 * d_inner].astype(BF16)
    hgrn_w_z_b = hgrn_w_in[:, :, 4 * d_inner:].astype(BF16)
    hgrn_w_out_b = hgrn_w_out.astype(BF16)
    assert depth % 2 == 0
    for layer in range(depth):
        j = layer // 2
        if layer % 2 == 0:
            x = _conv_layer(x, norm_w[layer], conv_w_in_b[j], conv_kernel[j], conv_w_out_b[j])
        else:
            lbc = jnp.clip(lower_bounds[j], 0.0, 1.0 - 1e-6).reshape(1, d_inner)
            lbf = jnp.maximum(lbc, LB_FLOOR)
            kadd = (lbf - lbc).reshape(n_heads, 1, HEAD_DIM)
            q, v, lff, lfb = _hgrn_in(x, norm_w[layer], hgrn_w_qffi_b[j], lbc, lbf)
            o = _hgrn_scan(q, v, lff, lfb, kadd)
            x = _hgrn_out(x, o, norm_w[layer], hgrn_w_z_b[j], hgrn_norm_w[j], hgrn_w_out_b[j],
                          final_norm_w, final_norm=(layer == depth - 1))
    return x
```

```python
import functools

import numpy as np
import jax
import jax.numpy as jnp
from jax import lax
from jax.experimental import pallas as pl
from jax.experimental.pallas import tpu as pltpu

HEAD_DIM = 128
EPS = 1e-6
LB_FLOOR = 1e-30
CONV_WIDTH = 3

V7X_LANES = 128
V7X_BF16_SUBLANES = 16
VMEM_LIMIT_BYTES = 56 * 1024 * 1024

SCAN_CHUNK = 64
SCAN_LEVELS = (32, 16, 8, 4, 2, 1)
SCAN_BCAST_LEVELS = 4
SCAN_FAST_RANGE = 86.0
SCAN_FAST_QMAX = 2.0 ** 20
SCAN_FAST_UNROLL = 4
LOG2_E = 1.4426950408889634

F32 = jnp.float32
BF16 = jnp.bfloat16


def _dot(a, b):
    return jnp.dot(a, b, preferred_element_type=F32)


def _dot_nt(a, b):
    return lax.dot_general(a, b, (((1,), (1,)), ((), ())), preferred_element_type=F32)


def _dot_tn(a, b):
    return lax.dot_general(a, b, (((0,), (0,)), ((), ())), preferred_element_type=F32)


def _rms_norm(xv, w):
    ms = jnp.mean(xv * xv, axis=-1, keepdims=True)
    return xv * lax.rsqrt(ms + EPS) * w


def _silu(z):
    return z * jax.nn.sigmoid(z)


def _const_spec(shape):
    nd = len(shape)
    return pl.BlockSpec(shape, lambda *_: (0,) * nd, pipeline_mode=pl.Buffered(1))


def _conv_layer_kernel(x_ref, xp_ref, xn_ref, nw_ref, win_ref, ck_ref, wout_ref, o_ref, h_scr,
                       *, tt, te, halo):
    i = pl.program_id(1)
    nt = pl.num_programs(1)
    d_inner = wout_ref.shape[0]
    nw = nw_ref[...]
    x = x_ref[0]
    h_scr[halo:halo + tt, :] = _rms_norm(x, nw).astype(BF16)
    h_scr[0:halo, :] = jnp.where(i > 0, _rms_norm(xp_ref[0], nw), 0.0).astype(BF16)
    h_scr[halo + tt:, :] = jnp.where(i < nt - 1, _rms_norm(xn_ref[0], nw), 0.0).astype(BF16)

    rows = tt + 2 * halo
    acc = jnp.zeros((tt, o_ref.shape[-1]), F32)
    for e in range(d_inner // te):
        def cols(g):
            return slice(g * d_inner + e * te, g * d_inner + (e + 1) * te)
        h_main = h_scr[halo:halo + tt, :]
        h_ext = h_scr[...]
        b_gate = _dot(h_main, win_ref[:, cols(0)])
        v = _dot(h_ext, win_ref[:, cols(1)]) * _dot(h_ext, win_ref[:, cols(2)])
        z = _dot(h_main, win_ref[:, cols(3)])
        ck = ck_ref[:, e * te:(e + 1) * te]
        v_prev = pltpu.roll(v, 1, 0)[halo:halo + tt]
        v_next = pltpu.roll(v, rows - 1, 0)[halo:halo + tt]
        conv = ck[0:1] * v_prev + ck[1:2] * v[halo:halo + tt] + ck[2:3] * v_next
        y = b_gate * conv * _silu(z)
        acc = acc + _dot(y.astype(BF16), wout_ref[e * te:(e + 1) * te, :])
    o_ref[0] = x + acc


def _conv_layer(x, norm_w, w_in, conv_k, w_out, *, tt=512, te=512):
    bsz, seq, d = x.shape
    d_inner = w_out.shape[0]
    halo = V7X_BF16_SUBLANES
    assert seq % tt == 0 and tt % halo == 0 and d_inner % te == 0
    nt = seq // tt
    hb = tt // halo
    last_hb = seq // halo - 1
    kern = functools.partial(_conv_layer_kernel, tt=tt, te=te, halo=halo)
    return pl.pallas_call(
        kern,
        name="conv_layer",
        grid=(bsz, nt),
        in_specs=[
            pl.BlockSpec((1, tt, d), lambda b, i: (b, i, 0)),
            pl.BlockSpec((1, halo, d), lambda b, i: (b, jnp.maximum(i * hb - 1, 0), 0)),
            pl.BlockSpec((1, halo, d), lambda b, i: (b, jnp.minimum((i + 1) * hb, last_hb), 0)),
            _const_spec((1, d)),
            _const_spec(w_in.shape),
            _const_spec(conv_k.shape),
            _const_spec(w_out.shape),
        ],
        out_specs=pl.BlockSpec((1, tt, d), lambda b, i: (b, i, 0)),
        out_shape=jax.ShapeDtypeStruct(x.shape, F32),
        scratch_shapes=[pltpu.VMEM((tt + 2 * halo, d), BF16)],
        compiler_params=pltpu.CompilerParams(
            dimension_semantics=("parallel", "arbitrary"), vmem_limit_bytes=VMEM_LIMIT_BYTES),
    )(x, x, x, norm_w.reshape(1, d), w_in, conv_k, w_out)


def _hgrn_in_kernel(x_ref, nw_ref, win_ref, lbc_ref, lbf_ref, q_ref, v_ref, lff_ref, lfb_ref, *, cw):
    d_inner = lbc_ref.shape[-1]
    h = _rms_norm(x_ref[0], nw_ref[...]).astype(BF16)
    hpc = cw // HEAD_DIM

    def proj(g, j):
        return _dot(h, win_ref[:, g * d_inner + j * cw:g * d_inner + (j + 1) * cw])

    def put_heads(ref, j, val):
        for i in range(hpc):
            ref[0, j * hpc + i] = val[:, i * HEAD_DIM:(i + 1) * HEAD_DIM].astype(ref.dtype)

    def log2_forget(fp, j):
        one_minus_lb = 1.0 - lbc_ref[:, j * cw:(j + 1) * cw]
        lb_floor = lbf_ref[:, j * cw:(j + 1) * cw]
        sig = 1.0 / (1.0 + jnp.exp(-fp))
        return jnp.log(lb_floor + one_minus_lb * sig) * LOG2_E

    for j in range(d_inner // cw):
        fp_fw = proj(1, j)
        put_heads(q_ref, j, proj(0, j) * (HEAD_DIM ** -0.5))
        put_heads(lff_ref, j, log2_forget(fp_fw, j))
        fp_bw = proj(2, j)
        put_heads(v_ref, j, proj(3, j))
        put_heads(lfb_ref, j, log2_forget(fp_bw, j))


def _hgrn_in(x, norm_w, w_qffi, lbc, lbf, *, tt=512, cw=256):
    bsz, seq, d = x.shape
    d_inner = lbc.shape[-1]
    n_heads = d_inner // HEAD_DIM
    assert seq % tt == 0 and d_inner % cw == 0 and cw % HEAD_DIM == 0 and w_qffi.shape == (d, 4 * d_inner)
    head_shape = (bsz, n_heads, seq, HEAD_DIM)
    head_spec = pl.BlockSpec((1, n_heads, tt, HEAD_DIM), lambda b, i: (b, 0, i, 0))
    return pl.pallas_call(
        functools.partial(_hgrn_in_kernel, cw=cw),
        name="hgrn_in",
        grid=(bsz, seq // tt),
        in_specs=[
            pl.BlockSpec((1, tt, d), lambda b, i: (b, i, 0)),
            _const_spec((1, d)),
            _const_spec(w_qffi.shape),
            _const_spec((1, d_inner)),
            _const_spec((1, d_inner)),
        ],
        out_specs=[head_spec] * 4,
        out_shape=[
            jax.ShapeDtypeStruct(head_shape, BF16),
            jax.ShapeDtypeStruct(head_shape, BF16),
            jax.ShapeDtypeStruct(head_shape, F32),
            jax.ShapeDtypeStruct(head_shape, F32),
        ],
        compiler_params=pltpu.CompilerParams(
            dimension_semantics=("parallel", "parallel"), vmem_limit_bytes=VMEM_LIMIT_BYTES),
    )(x, norm_w.reshape(1, d), w_qffi, lbc, lbf)


def _scan_constants():
    c = SCAN_CHUNK
    levels = SCAN_LEVELS
    nb = SCAN_BCAST_LEVELS
    tri = np.zeros((2, c, c), np.float32)
    qsel = np.zeros((2, len(levels), c, 1), np.float32)
    mask = np.zeros((2, len(levels) + 1, c, c), np.float32)
    coef = np.zeros((2, len(levels) - nb, 3, c, 1), np.float32)
    refs = [[[] for _ in range(nb)] for _ in range(2)]
    flip = lambda a: a[::-1, ::-1]
    tri[0] = np.tril(np.ones((c, c), np.float32))
    tri[1] = flip(tri[0])
    mask[0, 0] = mask[1, 0] = np.eye(c, dtype=np.float32)
    for li, m in enumerate(levels):
        expo = np.zeros((c, c), np.float32)
        lmask = np.zeros((c, c), np.float32)
        late = np.zeros((c,), np.float32)
        for r in range(c):
            blk, pos = divmod(r, 2 * m)
            ref = blk * 2 * m + m - 1
            if pos >= m:
                late[r] = 1.0
                expo[r, ref + 1:r + 1] = 1.0
                lmask[r, blk * 2 * m:blk * 2 * m + m] = 1.0
            else:
                expo[r, r + 1:ref + 1] = 1.0
        for d in range(2):
            ex = expo if d == 0 else flip(expo)
            qsel[d, li, :, 0] = late if d == 0 else late[::-1]
            mask[d, li + 1] = lmask if d == 0 else flip(lmask)
            if li < nb:
                for blk in range(c // (2 * m)):
                    ref = blk * 2 * m + m - 1
                    refs[d][li].append(ref if d == 0 else c - 1 - ref)
                if d == 1:
                    refs[d][li] = sorted(refs[d][li])
            else:
                tridiag = np.zeros_like(ex)
                for r in range(c):
                    for j, off in enumerate((-1, 0, 1)):
                        if 0 <= r + off < c:
                            coef[d, li - nb, j, r, 0] = ex[r, r + off]
                            tridiag[r, r + off] = ex[r, r + off]
                assert np.array_equal(tridiag, ex)
    lanes = lambda a: np.ascontiguousarray(np.broadcast_to(a, a.shape[:-1] + (HEAD_DIM,)))
    return tri, lanes(qsel), mask, lanes(coef), refs


def _chunk_cumsum(d, lf, tri_ref):
    hi = lf.astype(BF16)
    r1 = lf - hi.astype(F32)
    mid = r1.astype(BF16)
    lo = (r1 - mid.astype(F32)).astype(BF16)
    g3 = _dot(tri_ref[d], jnp.concatenate([hi, mid, lo], axis=1))
    return (g3[:, :HEAD_DIM] + g3[:, HEAD_DIM:2 * HEAD_DIM]) + g3[:, 2 * HEAD_DIM:]


def _key_from_log2f(lf, kadd):
    return (1.0 - jnp.exp2(lf)) + kadd


def _emit(acc_ref, o_ref, rows, out, final):
    if final:
        o_ref[rows, :] = (acc_ref[rows, :] + out).astype(o_ref.dtype)
    else:
        acc_ref[rows, :] = out


def _scan_pair_fast(chains, tri2_ref, causal2_ref, vsel_ref, final):
    c = SCAN_CHUNK
    half = c // 2
    dh = HEAD_DIM

    def per_chunk(fn):
        return jnp.concatenate([jnp.broadcast_to(fn(j), (c, dh)) for j in range(2)], axis=0)

    rows = [pl.ds(ch[8], 2 * c) for ch in chains]
    lf_l = [ch[4][rw, :] for ch, rw in zip(chains, rows)]
    g2 = []
    for ch, lf in zip(chains, lf_l):
        hi = lf.astype(BF16)
        mid = (lf - hi.astype(F32)).astype(BF16)
        g2.append(_dot(tri2_ref[ch[0]], jnp.concatenate([hi, mid], axis=1)))
    a_l, b_l, qg_l, kg_l, dcol_l = [], [], [], [], []
    for ch, rw, lf, gg in zip(chains, rows, lf_l, g2):
        d = ch[0]
        g = gg[:, :dh] + gg[:, dh:]
        mid_row = (lambda j: j * c + half - 1) if d == 0 else (lambda j: j * c + half)
        last_row = (lambda j: j * c + c - 1) if d == 0 else (lambda j: j * c)
        r_row = lambda j: g[mid_row(j):mid_row(j) + 1, :]
        gl_row = lambda j: g[last_row(j):last_row(j) + 1, :]
        dg = g - per_chunk(r_row)
        a = ch[1][rw, :] * jnp.exp2(dg).astype(BF16)
        b = _key_from_log2f(lf, ch[2]).astype(BF16) * jnp.exp2(-dg).astype(BF16)
        a_l.append(a)
        b_l.append(b)
        qg_l.append(a * per_chunk(lambda j: jnp.exp2(r_row(j))).astype(BF16))
        kg_l.append(b * per_chunk(lambda j: jnp.exp2(gl_row(j) - r_row(j))).astype(BF16))
        dcol_l.append([jnp.broadcast_to(jnp.exp2(gl_row(j)), (dh, dh)).T for j in range(2)])
    v_l = [ch[3][rw, :] for ch, rw in zip(chains, rows)]
    p_l = [_dot_nt(a, b) for a, b in zip(a_l, b_l)]
    u_l = [_dot_tn(kg, jnp.concatenate([v, v], axis=1) * vsel_ref[...]) for kg, v in zip(kg_l, v_l)]
    for ch, rw, p, u, qg, v, dcol in zip(chains, rows, p_l, u_l, qg_l, v_l, dcol_l):
        d = ch[0]
        sc = jnp.where(causal2_ref[d] > 0.5, p, 0.0).astype(BF16)
        s0 = ch[5][...]
        first, second = (0, 1) if d == 0 else (1, 0)
        s1 = dcol[first] * s0 + u[:, first * dh:(first + 1) * dh]
        ch[5][...] = dcol[second] * s1 + u[:, second * dh:(second + 1) * dh]
        s_by_chunk = (s0, s1) if d == 0 else (s1, s0)
        rhs = jnp.concatenate([
            jnp.concatenate([s_by_chunk[0].astype(BF16), s_by_chunk[1].astype(BF16)], axis=1),
            jnp.concatenate([v, v], axis=1)], axis=0)
        o2 = _dot(jnp.concatenate([qg, sc], axis=1), rhs)
        _emit(ch[6], ch[7], rw, jnp.concatenate([o2[:c, :dh], o2[c:, dh:]], axis=0), final)


def _scan_chunk(d, refs, q_ref, kadd, v_ref, lf_ref, tri_ref, qsel_ref, mask_ref, coef_ref,
                s_ref, acc_ref, o_ref, row0, final):
    c = SCAN_CHUNK
    nb = SCAN_BCAST_LEVELS
    rows = pl.ds(row0, c)
    q_b = q_ref[rows, :]
    v_b = v_ref[rows, :]
    lf = lf_ref[rows, :]
    q = q_b.astype(F32)
    k = _key_from_log2f(lf, kadd)
    g = _chunk_cumsum(d, lf, tri_ref)
    g_last = g[c - 1:c, :] if d == 0 else g[0:1, :]

    state = s_ref[...]
    qg = (q * jnp.exp2(g)).astype(BF16)
    out = _dot(qg, state.astype(BF16))

    scores = mask_ref[d, 0] * _dot_nt(q_b, k.astype(BF16))
    lf_dn = pltpu.roll(lf, 1, 0)
    lf_up = pltpu.roll(lf, c - 1, 0)
    for li, m in enumerate(SCAN_LEVELS):
        late = qsel_ref[d, li] > 0.5
        if li < nb:
            g_blk = jnp.concatenate(
                [jnp.broadcast_to(g[r:r + 1, :], (2 * m, HEAD_DIM)) for r in refs[d][li]], axis=0)
            expo = jnp.where(late, g - g_blk, g_blk - g)
        else:
            cf = coef_ref[d, li - nb]
            expo = cf[0] * lf_dn + cf[1] * lf + cf[2] * lf_up
        xx = (jnp.where(late, q, k) * jnp.exp2(expo)).astype(BF16)
        scores = scores + mask_ref[d, li + 1] * _dot_nt(xx, xx)
    out = out + _dot(scores.astype(BF16), v_b)
    _emit(acc_ref, o_ref, rows, out, final)

    kg = (k * jnp.exp2(g_last - g)).astype(BF16)
    decay = jnp.broadcast_to(jnp.exp2(g_last), (HEAD_DIM, HEAD_DIM)).T
    s_ref[...] = decay * state + _dot_tn(kg, v_b)


def _hgrn_scan_kernel(q_ref, v_ref, lff_ref, lfb_ref, kadd_ref,
                      tri_ref, tri2_ref, causal2_ref, vsel_ref, qsel_ref, mask_ref, coef_ref, o_ref,
                      s_scr, acc_scr, *, hp, refs):
    seq = q_ref.shape[2]
    c = SCAN_CHUNK
    nc = seq // c
    n_pairs = nc // 2
    n_trips = n_pairs // SCAN_FAST_UNROLL
    s_scr[...] = jnp.zeros(s_scr.shape, F32)

    def fast_body(final, ti, carry):
        chains = []
        for j in range(SCAN_FAST_UNROLL):
            pi = ti * SCAN_FAST_UNROLL + j
            row_f = pl.multiple_of(pi * 2 * c, 2 * c)
            row_b = pl.multiple_of((n_pairs - 1 - pi) * 2 * c, 2 * c)
            for hd in range(hp):
                chains.append((0, q_ref.at[0, hd], kadd_ref[hd], v_ref.at[0, hd], lff_ref.at[0, hd],
                               s_scr.at[hd], acc_scr.at[hd], o_ref.at[0, hd], row_f))
                chains.append((1, q_ref.at[0, hd], kadd_ref[hd], v_ref.at[0, hd], lfb_ref.at[0, hd],
                               s_scr.at[hp + hd], acc_scr.at[hd], o_ref.at[0, hd], row_b))
        _scan_pair_fast(chains, tri2_ref, causal2_ref, vsel_ref, final)
        return carry

    def robust_body(final, ci, carry):
        row_f = pl.multiple_of(ci * c, c)
        row_b = pl.multiple_of((nc - 1 - ci) * c, c)
        for hd in range(hp):
            _scan_chunk(0, refs, q_ref.at[0, hd], kadd_ref[hd], v_ref.at[0, hd], lff_ref.at[0, hd],
                        tri_ref, qsel_ref, mask_ref, coef_ref,
                        s_scr.at[hd], acc_scr.at[hd], o_ref.at[0, hd], row_f, final)
            _scan_chunk(1, refs, q_ref.at[0, hd], kadd_ref[hd], v_ref.at[0, hd], lfb_ref.at[0, hd],
                        tri_ref, qsel_ref, mask_ref, coef_ref,
                        s_scr.at[hp + hd], acc_scr.at[hd], o_ref.at[0, hd], row_b, final)
        return carry

    half = c // 2
    min_half = jnp.float32(0.0)
    max_q = jnp.float32(0.0)
    for hd in range(hp):
        for lf_ref in (lff_ref, lfb_ref):
            sums = jnp.sum(lf_ref[0, hd].reshape(seq // half, half, HEAD_DIM), axis=1)
            min_half = jnp.minimum(min_half, jnp.min(sums))
        max_q = jnp.maximum(max_q, jnp.max(jnp.abs(q_ref[0, hd].astype(F32))))
    fast_ok = jnp.logical_and(min_half >= -SCAN_FAST_RANGE, max_q <= SCAN_FAST_QMAX)

    @pl.when(fast_ok)
    def _():
        lax.fori_loop(0, n_trips // 2, functools.partial(fast_body, False), 0)
        lax.fori_loop(n_trips // 2, n_trips, functools.partial(fast_body, True), 0)

    @pl.when(jnp.logical_not(fast_ok))
    def _():
        lax.fori_loop(0, nc // 2, functools.partial(robust_body, False), 0)
        lax.fori_loop(nc // 2, nc, functools.partial(robust_body, True), 0)


def _hgrn_scan(q, v, lff, lfb, kadd, *, hp=2):
    bsz, n_heads, seq, dh = q.shape
    assert dh == HEAD_DIM and n_heads % hp == 0
    assert seq % (4 * SCAN_CHUNK * SCAN_FAST_UNROLL) == 0
    tri, qsel, mask, coef, refs = _scan_constants()
    c = SCAN_CHUNK
    tri2 = np.zeros((2, 2 * c, 2 * c), np.float32)
    tri2[:, :c, :c] = tri
    tri2[:, c:, c:] = tri
    vsel = np.zeros((2 * c, 2 * dh), np.float32)
    vsel[:c, :dh] = 1.0
    vsel[c:, dh:] = 1.0
    consts = (jnp.asarray(tri, BF16), jnp.asarray(tri2, BF16), jnp.asarray(tri2), jnp.asarray(vsel, BF16),
              jnp.asarray(qsel), jnp.asarray(mask), jnp.asarray(coef))
    head_spec = pl.BlockSpec((1, hp, seq, dh), lambda b, h: (b, h, 0, 0))
    kern = functools.partial(_hgrn_scan_kernel, hp=hp, refs=refs)
    return pl.pallas_call(
        kern,
        name="hgrn_scan",
        grid=(bsz, n_heads // hp),
        in_specs=[head_spec] * 4 + [pl.BlockSpec((hp, 1, dh), lambda b, h: (h, 0, 0))]
        + [_const_spec(a.shape) for a in consts],
        out_specs=head_spec,
        out_shape=jax.ShapeDtypeStruct(q.shape, BF16),
        scratch_shapes=[
            pltpu.VMEM((2 * hp, dh, dh), F32),
            pltpu.VMEM((hp, seq, dh), F32),
        ],
        compiler_params=pltpu.CompilerParams(
            dimension_semantics=("parallel", "parallel"), vmem_limit_bytes=VMEM_LIMIT_BYTES),
    )(q, v, lff, lfb, kadd, *consts)


def _hgrn_out_kernel(x_ref, o_ref, nw_ref, wz_ref, hnw_ref, wout_ref, fnw_ref, out_ref, *, final_norm):
    n_heads = o_ref.shape[1]
    x = x_ref[0]
    z = _dot(_rms_norm(x, nw_ref[...]).astype(BF16), wz_ref[...])
    heads = []
    for hd in range(n_heads):
        o = o_ref[0, hd].astype(F32)
        heads.append(o * lax.rsqrt(jnp.mean(o * o, axis=-1, keepdims=True) + EPS))
    y = jnp.concatenate(heads, axis=-1) * hnw_ref[...] * _silu(z)
    res = x + _dot(y.astype(BF16), wout_ref[...])
    if final_norm:
        res = _rms_norm(res, fnw_ref[...])
    out_ref[0] = res


def _hgrn_out(x, o, norm_w, w_z, head_norm_w, w_out, final_norm_w, *, final_norm, tt=512):
    bsz, seq, d = x.shape
    n_heads = o.shape[1]
    d_inner = w_out.shape[0]
    assert seq % tt == 0 and w_z.shape == (d, d_inner)
    kern = functools.partial(_hgrn_out_kernel, final_norm=final_norm)
    return pl.pallas_call(
        kern,
        name="hgrn_out",
        grid=(bsz, seq // tt),
        in_specs=[
            pl.BlockSpec((1, tt, d), lambda b, i: (b, i, 0)),
            pl.BlockSpec((1, n_heads, tt, HEAD_DIM), lambda b, i: (b, 0, i, 0)),
            _const_spec((1, d)),
            _const_spec(w_z.shape),
            _const_spec((1, d_inner)),
            _const_spec(w_out.shape),
            _const_spec((1, d)),
        ],
        out_specs=pl.BlockSpec((1, tt, d), lambda b, i: (b, i, 0)),
        out_shape=jax.ShapeDtypeStruct(x.shape, F32),
        compiler_params=pltpu.CompilerParams(
            dimension_semantics=("parallel", "parallel"), vmem_limit_bytes=VMEM_LIMIT_BYTES),
    )(x, o, norm_w.reshape(1, d), w_z, head_norm_w.reshape(1, d_inner), w_out, final_norm_w.reshape(1, d))


def _hgrn_lower_bounds(lb_logits):
    p = jax.nn.softmax(lb_logits.astype(F32), axis=0)
    return jnp.cumsum(p, axis=0) - p[0]


def kernel(x, norm_w, final_norm_w, conv_w_in, conv_kernel, conv_w_out,
           hgrn_w_in, hgrn_lb_logits, hgrn_norm_w, hgrn_w_out):
    depth = norm_w.shape[0]
    d_inner = hgrn_w_out.shape[1]
    n_heads = d_inner // HEAD_DIM
    lower_bounds = _hgrn_lower_bounds(hgrn_lb_logits)
    conv_w_in_b = conv_w_in.astype(BF16)
    conv_w_out_b = conv_w_out.astype(BF16)
    hgrn_w_qffi_b = hgrn_w_in[:, :, :4 * d_inner].astype(BF16)
    hgrn_w_z_b = hgrn_w_in[:, :, 4 * d_inner:].astype(BF16)
    hgrn_w_out_b = hgrn_w_out.astype(BF16)
    assert depth % 2 == 0
    for layer in range(depth):
        j = layer // 2
        if layer % 2 == 0:
            x = _conv_layer(x, norm_w[layer], conv_w_in_b[j], conv_kernel[j], conv_w_out_b[j])
        else:
            lbc = jnp.clip(lower_bounds[j], 0.0, 1.0 - 1e-6).reshape(1, d_inner)
            lbf = jnp.maximum(lbc, LB_FLOOR)
            kadd = (lbf - lbc).reshape(n_heads, 1, HEAD_DIM)
            q, v, lff, lfb = _hgrn_in(x, norm_w[layer], hgrn_w_qffi_b[j], lbc, lbf)
            o = _hgrn_scan(q, v, lff, lfb, kadd)
            x = _hgrn_out(x, o, norm_w[layer], hgrn_w_z_b[j], hgrn_norm_w[j], hgrn_w_out_b[j],
                          final_norm_w, final_norm=(layer == depth - 1))
    return x
```

```python
import functools

import numpy as np
import jax
import jax.numpy as jnp
from jax import lax
from jax.experimental import pallas as pl
from jax.experimental.pallas import tpu as pltpu

HEAD_DIM = 128
EPS = 1e-6
LB_FLOOR = 1e-30
CONV_WIDTH = 3

V7X_LANES = 128
V7X_BF16_SUBLANES = 16
VMEM_LIMIT_BYTES = 56 * 1024 * 1024

SCAN_CHUNK = 64
SCAN_LEVELS = (32, 16, 8, 4, 2, 1)
SCAN_BCAST_LEVELS = 4
SCAN_FAST_RANGE = 86.0
SCAN_FAST_QMAX = 2.0 ** 20
Q_BOUND_SLACK = 1.02
SCAN_FAST_UNROLL = 4
LOG2_E = 1.4426950408889634

F32 = jnp.float32
BF16 = jnp.bfloat16


def _dot(a, b):
    return jnp.dot(a, b, preferred_element_type=F32)


def _dot_nt(a, b):
    return lax.dot_general(a, b, (((1,), (1,)), ((), ())), preferred_element_type=F32)


def _dot_tn(a, b):
    return lax.dot_general(a, b, (((0,), (0,)), ((), ())), preferred_element_type=F32)


def _rms_norm(xv, w):
    ms = jnp.mean(xv * xv, axis=-1, keepdims=True)
    return xv * lax.rsqrt(ms + EPS) * w


def _silu(z):
    return z * jax.nn.sigmoid(z)


def _const_spec(shape):
    nd = len(shape)
    return pl.BlockSpec(shape, lambda *_: (0,) * nd, pipeline_mode=pl.Buffered(1))


def _conv_layer_kernel(x_ref, xp_ref, xn_ref, nw_ref, win_ref, ck_ref, wout_ref, o_ref, h_scr,
                       *, tt, te, halo):
    i = pl.program_id(1)
    nt = pl.num_programs(1)
    d_inner = wout_ref.shape[0]
    nw = nw_ref[...]
    x = x_ref[0]
    h_scr[halo:halo + tt, :] = _rms_norm(x, nw).astype(BF16)
    h_scr[0:halo, :] = jnp.where(i > 0, _rms_norm(xp_ref[0], nw), 0.0).astype(BF16)
    h_scr[halo + tt:, :] = jnp.where(i < nt - 1, _rms_norm(xn_ref[0], nw), 0.0).astype(BF16)

    rows = tt + 2 * halo
    acc = jnp.zeros((tt, o_ref.shape[-1]), F32)
    for e in range(d_inner // te):
        def cols(g):
            return slice(g * d_inner + e * te, g * d_inner + (e + 1) * te)
        h_main = h_scr[halo:halo + tt, :]
        h_ext = h_scr[...]
        b_gate = _dot(h_main, win_ref[:, cols(0)])
        v = _dot(h_ext, win_ref[:, cols(1)]) * _dot(h_ext, win_ref[:, cols(2)])
        z = _dot(h_main, win_ref[:, cols(3)])
        ck = ck_ref[:, e * te:(e + 1) * te]
        v_prev = pltpu.roll(v, 1, 0)[halo:halo + tt]
        v_next = pltpu.roll(v, rows - 1, 0)[halo:halo + tt]
        conv = ck[0:1] * v_prev + ck[1:2] * v[halo:halo + tt] + ck[2:3] * v_next
        y = b_gate * conv * _silu(z)
        acc = acc + _dot(y.astype(BF16), wout_ref[e * te:(e + 1) * te, :])
    o_ref[0] = x + acc


def _conv_layer(x, norm_w, w_in, conv_k, w_out, *, tt=512, te=512):
    bsz, seq, d = x.shape
    d_inner = w_out.shape[0]
    halo = V7X_BF16_SUBLANES
    assert seq % tt == 0 and tt % halo == 0 and d_inner % te == 0
    nt = seq // tt
    hb = tt // halo
    last_hb = seq // halo - 1
    kern = functools.partial(_conv_layer_kernel, tt=tt, te=te, halo=halo)
    return pl.pallas_call(
        kern,
        name="conv_layer",
        grid=(bsz, nt),
        in_specs=[
            pl.BlockSpec((1, tt, d), lambda b, i: (b, i, 0)),
            pl.BlockSpec((1, halo, d), lambda b, i: (b, jnp.maximum(i * hb - 1, 0), 0)),
            pl.BlockSpec((1, halo, d), lambda b, i: (b, jnp.minimum((i + 1) * hb, last_hb), 0)),
            _const_spec((1, d)),
            _const_spec(w_in.shape),
            _const_spec(conv_k.shape),
            _const_spec(w_out.shape),
        ],
        out_specs=pl.BlockSpec((1, tt, d), lambda b, i: (b, i, 0)),
        out_shape=jax.ShapeDtypeStruct(x.shape, F32),
        scratch_shapes=[pltpu.VMEM((tt + 2 * halo, d), BF16)],
        compiler_params=pltpu.CompilerParams(
            dimension_semantics=("parallel", "arbitrary"), vmem_limit_bytes=VMEM_LIMIT_BYTES),
    )(x, x, x, norm_w.reshape(1, d), w_in, conv_k, w_out)


def _hgrn_in_kernel(x_ref, nw_ref, win_ref, lbc_ref, lbf_ref, q_ref, v_ref, lff_ref, lfb_ref, *, cw):
    d_inner = lbc_ref.shape[-1]
    h = _rms_norm(x_ref[0], nw_ref[...]).astype(BF16)
    hpc = cw // HEAD_DIM

    def proj(g, j):
        return _dot(h, win_ref[:, g * d_inner + j * cw:g * d_inner + (j + 1) * cw])

    def put_heads(ref, j, val):
        for i in range(hpc):
            ref[0, j * hpc + i] = val[:, i * HEAD_DIM:(i + 1) * HEAD_DIM].astype(ref.dtype)

    def log2_forget(fp, j):
        one_minus_lb = 1.0 - lbc_ref[:, j * cw:(j + 1) * cw]
        lb_floor = lbf_ref[:, j * cw:(j + 1) * cw]
        sig = 1.0 / (1.0 + jnp.exp(-fp))
        return jnp.log(lb_floor + one_minus_lb * sig) * LOG2_E

    for j in range(d_inner // cw):
        fp_fw = proj(1, j)
        put_heads(q_ref, j, proj(0, j) * (HEAD_DIM ** -0.5))
        put_heads(lff_ref, j, log2_forget(fp_fw, j))
        fp_bw = proj(2, j)
        put_heads(v_ref, j, proj(3, j))
        put_heads(lfb_ref, j, log2_forget(fp_bw, j))


def _hgrn_in(x, norm_w, w_qffi, lbc, lbf, *, tt=512, cw=256):
    bsz, seq, d = x.shape
    d_inner = lbc.shape[-1]
    n_heads = d_inner // HEAD_DIM
    assert seq % tt == 0 and d_inner % cw == 0 and cw % HEAD_DIM == 0 and w_qffi.shape == (d, 4 * d_inner)
    head_shape = (bsz, n_heads, seq, HEAD_DIM)
    head_spec = pl.BlockSpec((1, n_heads, tt, HEAD_DIM), lambda b, i: (b, 0, i, 0))
    return pl.pallas_call(
        functools.partial(_hgrn_in_kernel, cw=cw),
        name="hgrn_in",
        grid=(bsz, seq // tt),
        in_specs=[
            pl.BlockSpec((1, tt, d), lambda b, i: (b, i, 0)),
            _const_spec((1, d)),
            _const_spec(w_qffi.shape),
            _const_spec((1, d_inner)),
            _const_spec((1, d_inner)),
        ],
        out_specs=[head_spec] * 4,
        out_shape=[
            jax.ShapeDtypeStruct(head_shape, BF16),
            jax.ShapeDtypeStruct(head_shape, BF16),
            jax.ShapeDtypeStruct(head_shape, F32),
            jax.ShapeDtypeStruct(head_shape, F32),
        ],
        compiler_params=pltpu.CompilerParams(
            dimension_semantics=("parallel", "parallel"), vmem_limit_bytes=VMEM_LIMIT_BYTES),
    )(x, norm_w.reshape(1, d), w_qffi, lbc, lbf)


def _scan_constants():
    c = SCAN_CHUNK
    levels = SCAN_LEVELS
    nb = SCAN_BCAST_LEVELS
    tri = np.zeros((2, c, c), np.float32)
    qsel = np.zeros((2, len(levels), c, 1), np.float32)
    mask = np.zeros((2, len(levels) + 1, c, c), np.float32)
    coef = np.zeros((2, len(levels) - nb, 3, c, 1), np.float32)
    refs = [[[] for _ in range(nb)] for _ in range(2)]
    flip = lambda a: a[::-1, ::-1]
    tri[0] = np.tril(np.ones((c, c), np.float32))
    tri[1] = flip(tri[0])
    mask[0, 0] = mask[1, 0] = np.eye(c, dtype=np.float32)
    for li, m in enumerate(levels):
        expo = np.zeros((c, c), np.float32)
        lmask = np.zeros((c, c), np.float32)
        late = np.zeros((c,), np.float32)
        for r in range(c):
            blk, pos = divmod(r, 2 * m)
            ref = blk * 2 * m + m - 1
            if pos >= m:
                late[r] = 1.0
                expo[r, ref + 1:r + 1] = 1.0
                lmask[r, blk * 2 * m:blk * 2 * m + m] = 1.0
            else:
                expo[r, r + 1:ref + 1] = 1.0
        for d in range(2):
            ex = expo if d == 0 else flip(expo)
            qsel[d, li, :, 0] = late if d == 0 else late[::-1]
            mask[d, li + 1] = lmask if d == 0 else flip(lmask)
            if li < nb:
                for blk in range(c // (2 * m)):
                    ref = blk * 2 * m + m - 1
                    refs[d][li].append(ref if d == 0 else c - 1 - ref)
                if d == 1:
                    refs[d][li] = sorted(refs[d][li])
            else:
                tridiag = np.zeros_like(ex)
                for r in range(c):
                    for j, off in enumerate((-1, 0, 1)):
                        if 0 <= r + off < c:
                            coef[d, li - nb, j, r, 0] = ex[r, r + off]
                            tridiag[r, r + off] = ex[r, r + off]
                assert np.array_equal(tridiag, ex)
    lanes = lambda a: np.ascontiguousarray(np.broadcast_to(a, a.shape[:-1] + (HEAD_DIM,)))
    return tri, lanes(qsel), mask, lanes(coef), refs


def _chunk_cumsum(d, lf, tri_ref):
    hi = lf.astype(BF16)
    r1 = lf - hi.astype(F32)
    mid = r1.astype(BF16)
    lo = (r1 - mid.astype(F32)).astype(BF16)
    g3 = _dot(tri_ref[d], jnp.concatenate([hi, mid, lo], axis=1))
    return (g3[:, :HEAD_DIM] + g3[:, HEAD_DIM:2 * HEAD_DIM]) + g3[:, 2 * HEAD_DIM:]


def _key_from_log2f(lf, kadd):
    return (1.0 - jnp.exp2(lf)) + kadd


def _emit(acc_ref, o_ref, rows, out, final):
    if final:
        o_ref[rows, :] = (acc_ref[rows, :] + out).astype(o_ref.dtype)
    else:
        acc_ref[rows, :] = out


def _scan_pair_fast(chains, tri2_ref, causal2_ref, vsel_ref, final, worst):
    c = SCAN_CHUNK
    half = c // 2
    dh = HEAD_DIM

    def per_chunk(fn):
        return jnp.concatenate([jnp.broadcast_to(fn(j), (c, dh)) for j in range(2)], axis=0)

    rows = [pl.ds(ch[8], 2 * c) for ch in chains]
    lf_l = [ch[4][rw, :] for ch, rw in zip(chains, rows)]
    g2 = []
    for ch, lf in zip(chains, lf_l):
        hi = lf.astype(BF16)
        mid = (lf - hi.astype(F32)).astype(BF16)
        g2.append(_dot(tri2_ref[ch[0]], jnp.concatenate([hi, mid], axis=1)))
    a_l, b_l, qg_l, kg_l, dcol_l = [], [], [], [], []
    for ch, rw, lf, gg in zip(chains, rows, lf_l, g2):
        d = ch[0]
        g = gg[:, :dh] + gg[:, dh:]
        mid_row = (lambda j: j * c + half - 1) if d == 0 else (lambda j: j * c + half)
        last_row = (lambda j: j * c + c - 1) if d == 0 else (lambda j: j * c)
        r_row = lambda j: g[mid_row(j):mid_row(j) + 1, :]
        gl_row = lambda j: g[last_row(j):last_row(j) + 1, :]
        for j in range(2):
            worst = jnp.minimum(worst, jnp.minimum(r_row(j), gl_row(j) - r_row(j)))
        dg = g - per_chunk(r_row)
        a = ch[1][rw, :] * jnp.exp2(dg).astype(BF16)
        b = _key_from_log2f(lf, ch[2]).astype(BF16) * jnp.exp2(-dg).astype(BF16)
        a_l.append(a)
        b_l.append(b)
        qg_l.append(a * per_chunk(lambda j: jnp.exp2(r_row(j))).astype(BF16))
        kg_l.append(b * per_chunk(lambda j: jnp.exp2(gl_row(j) - r_row(j))).astype(BF16))
        dcol_l.append([jnp.broadcast_to(jnp.exp2(gl_row(j)), (dh, dh)).T for j in range(2)])
    v_l = [ch[3][rw, :] for ch, rw in zip(chains, rows)]
    p_l = [_dot_nt(a, b) for a, b in zip(a_l, b_l)]
    u_l = [_dot_tn(kg, jnp.concatenate([v, v], axis=1) * vsel_ref[...]) for kg, v in zip(kg_l, v_l)]
    for ch, rw, p, u, qg, v, dcol in zip(chains, rows, p_l, u_l, qg_l, v_l, dcol_l):
        d = ch[0]
        sc = jnp.where(causal2_ref[d] > 0.5, p, 0.0).astype(BF16)
        s0 = ch[5][...]
        first, second = (0, 1) if d == 0 else (1, 0)
        s1 = dcol[first] * s0 + u[:, first * dh:(first + 1) * dh]
        ch[5][...] = dcol[second] * s1 + u[:, second * dh:(second + 1) * dh]
        s_by_chunk = (s0, s1) if d == 0 else (s1, s0)
        rhs = jnp.concatenate([
            jnp.concatenate([s_by_chunk[0].astype(BF16), s_by_chunk[1].astype(BF16)], axis=1),
            jnp.concatenate([v, v], axis=1)], axis=0)
        o2 = _dot(jnp.concatenate([qg, sc], axis=1), rhs)
        _emit(ch[6], ch[7], rw, jnp.concatenate([o2[:c, :dh], o2[c:, dh:]], axis=0), final)
    return worst


def _scan_chunk(d, refs, q_ref, kadd, v_ref, lf_ref, tri_ref, qsel_ref, mask_ref, coef_ref,
                s_ref, acc_ref, o_ref, row0, final):
    c = SCAN_CHUNK
    nb = SCAN_BCAST_LEVELS
    rows = pl.ds(row0, c)
    q_b = q_ref[rows, :]
    v_b = v_ref[rows, :]
    lf = lf_ref[rows, :]
    q = q_b.astype(F32)
    k = _key_from_log2f(lf, kadd)
    g = _chunk_cumsum(d, lf, tri_ref)
    g_last = g[c - 1:c, :] if d == 0 else g[0:1, :]

    state = s_ref[...]
    qg = (q * jnp.exp2(g)).astype(BF16)
    out = _dot(qg, state.astype(BF16))

    scores = mask_ref[d, 0] * _dot_nt(q_b, k.astype(BF16))
    lf_dn = pltpu.roll(lf, 1, 0)
    lf_up = pltpu.roll(lf, c - 1, 0)
    for li, m in enumerate(SCAN_LEVELS):
        late = qsel_ref[d, li] > 0.5
        if li < nb:
            g_blk = jnp.concatenate(
                [jnp.broadcast_to(g[r:r + 1, :], (2 * m, HEAD_DIM)) for r in refs[d][li]], axis=0)
            expo = jnp.where(late, g - g_blk, g_blk - g)
        else:
            cf = coef_ref[d, li - nb]
            expo = cf[0] * lf_dn + cf[1] * lf + cf[2] * lf_up
        xx = (jnp.where(late, q, k) * jnp.exp2(expo)).astype(BF16)
        scores = scores + mask_ref[d, li + 1] * _dot_nt(xx, xx)
    out = out + _dot(scores.astype(BF16), v_b)
    _emit(acc_ref, o_ref, rows, out, final)

    kg = (k * jnp.exp2(g_last - g)).astype(BF16)
    decay = jnp.broadcast_to(jnp.exp2(g_last), (HEAD_DIM, HEAD_DIM)).T
    s_ref[...] = decay * state + _dot_tn(kg, v_b)


def _hgrn_scan_kernel(q_ref, v_ref, lff_ref, lfb_ref, kadd_ref, qbound_ref,
                      tri_ref, tri2_ref, causal2_ref, vsel_ref, qsel_ref, mask_ref, coef_ref, o_ref,
                      s_scr, acc_scr, *, hp, refs):
    seq = q_ref.shape[2]
    c = SCAN_CHUNK
    nc = seq // c
    n_pairs = nc // 2
    n_trips = n_pairs // SCAN_FAST_UNROLL
    s_scr[...] = jnp.zeros(s_scr.shape, F32)

    def fast_body(final, ti, carry):
        chains = []
        for j in range(SCAN_FAST_UNROLL):
            pi = ti * SCAN_FAST_UNROLL + j
            row_f = pl.multiple_of(pi * 2 * c, 2 * c)
            row_b = pl.multiple_of((n_pairs - 1 - pi) * 2 * c, 2 * c)
            for hd in range(hp):
                chains.append((0, q_ref.at[0, hd], kadd_ref[hd], v_ref.at[0, hd], lff_ref.at[0, hd],
                               s_scr.at[hd], acc_scr.at[hd], o_ref.at[0, hd], row_f))
                chains.append((1, q_ref.at[0, hd], kadd_ref[hd], v_ref.at[0, hd], lfb_ref.at[0, hd],
                               s_scr.at[hp + hd], acc_scr.at[hd], o_ref.at[0, hd], row_b))
        return _scan_pair_fast(chains, tri2_ref, causal2_ref, vsel_ref, final, carry)

    def robust_body(final, ci, carry):
        row_f = pl.multiple_of(ci * c, c)
        row_b = pl.multiple_of((nc - 1 - ci) * c, c)
        for hd in range(hp):
            _scan_chunk(0, refs, q_ref.at[0, hd], kadd_ref[hd], v_ref.at[0, hd], lff_ref.at[0, hd],
                        tri_ref, qsel_ref, mask_ref, coef_ref,
                        s_scr.at[hd], acc_scr.at[hd], o_ref.at[0, hd], row_f, final)
            _scan_chunk(1, refs, q_ref.at[0, hd], kadd_ref[hd], v_ref.at[0, hd], lfb_ref.at[0, hd],
                        tri_ref, qsel_ref, mask_ref, coef_ref,
                        s_scr.at[hp + hd], acc_scr.at[hd], o_ref.at[0, hd], row_b, final)
        return carry

    worst = jnp.zeros((1, HEAD_DIM), F32)
    worst = lax.fori_loop(0, n_trips // 2, functools.partial(fast_body, False), worst)
    worst = lax.fori_loop(n_trips // 2, n_trips, functools.partial(fast_body, True), worst)
    fast_ok = jnp.logical_and(jnp.min(worst) >= -SCAN_FAST_RANGE,
                              jnp.max(qbound_ref[...]) <= SCAN_FAST_QMAX)

    @pl.when(jnp.logical_not(fast_ok))
    def _():
        s_scr[...] = jnp.zeros(s_scr.shape, F32)
        lax.fori_loop(0, nc // 2, functools.partial(robust_body, False), 0)
        lax.fori_loop(nc // 2, nc, functools.partial(robust_body, True), 0)


def _hgrn_scan(q, v, lff, lfb, kadd, qbound, *, hp=2):
    bsz, n_heads, seq, dh = q.shape
    assert dh == HEAD_DIM and n_heads % hp == 0
    assert seq % (4 * SCAN_CHUNK * SCAN_FAST_UNROLL) == 0
    tri, qsel, mask, coef, refs = _scan_constants()
    c = SCAN_CHUNK
    tri2 = np.zeros((2, 2 * c, 2 * c), np.float32)
    tri2[:, :c, :c] = tri
    tri2[:, c:, c:] = tri
    vsel = np.zeros((2 * c, 2 * dh), np.float32)
    vsel[:c, :dh] = 1.0
    vsel[c:, dh:] = 1.0
    consts = (jnp.asarray(tri, BF16), jnp.asarray(tri2, BF16), jnp.asarray(tri2), jnp.asarray(vsel, BF16),
              jnp.asarray(qsel), jnp.asarray(mask), jnp.asarray(coef))
    head_spec = pl.BlockSpec((1, hp, seq, dh), lambda b, h: (b, h, 0, 0))
    kern = functools.partial(_hgrn_scan_kernel, hp=hp, refs=refs)
    return pl.pallas_call(
        kern,
        name="hgrn_scan",
        grid=(bsz, n_heads // hp),
        in_specs=[head_spec] * 4 + [pl.BlockSpec((hp, 1, dh), lambda b, h: (h, 0, 0))] * 2
        + [_const_spec(a.shape) for a in consts],
        out_specs=head_spec,
        out_shape=jax.ShapeDtypeStruct(q.shape, BF16),
        scratch_shapes=[
            pltpu.VMEM((2 * hp, dh, dh), F32),
            pltpu.VMEM((hp, seq, dh), F32),
        ],
        compiler_params=pltpu.CompilerParams(
            dimension_semantics=("parallel", "parallel"), vmem_limit_bytes=VMEM_LIMIT_BYTES),
    )(q, v, lff, lfb, kadd, qbound, *consts)


def _hgrn_out_kernel(x_ref, o_ref, nw_ref, wz_ref, hnw_ref, wout_ref, fnw_ref, out_ref, *, final_norm):
    n_heads = o_ref.shape[1]
    x = x_ref[0]
    z = _dot(_rms_norm(x, nw_ref[...]).astype(BF16), wz_ref[...])
    heads = []
    for hd in range(n_heads):
        o = o_ref[0, hd].astype(F32)
        heads.append(o * lax.rsqrt(jnp.mean(o * o, axis=-1, keepdims=True) + EPS))
    y = jnp.concatenate(heads, axis=-1) * hnw_ref[...] * _silu(z)
    res = x + _dot(y.astype(BF16), wout_ref[...])
    if final_norm:
        res = _rms_norm(res, fnw_ref[...])
    out_ref[0] = res


def _hgrn_out(x, o, norm_w, w_z, head_norm_w, w_out, final_norm_w, *, final_norm, tt=512):
    bsz, seq, d = x.shape
    n_heads = o.shape[1]
    d_inner = w_out.shape[0]
    assert seq % tt == 0 and w_z.shape == (d, d_inner)
    kern = functools.partial(_hgrn_out_kernel, final_norm=final_norm)
    return pl.pallas_call(
        kern,
        name="hgrn_out",
        grid=(bsz, seq // tt),
        in_specs=[
            pl.BlockSpec((1, tt, d), lambda b, i: (b, i, 0)),
            pl.BlockSpec((1, n_heads, tt, HEAD_DIM), lambda b, i: (b, 0, i, 0)),
            _const_spec((1, d)),
            _const_spec(w_z.shape),
            _const_spec((1, d_inner)),
            _const_spec(w_out.shape),
            _const_spec((1, d)),
        ],
        out_specs=pl.BlockSpec((1, tt, d), lambda b, i: (b, i, 0)),
        out_shape=jax.ShapeDtypeStruct(x.shape, F32),
        compiler_params=pltpu.CompilerParams(
            dimension_semantics=("parallel", "parallel"), vmem_limit_bytes=VMEM_LIMIT_BYTES),
    )(x, o, norm_w.reshape(1, d), w_z, head_norm_w.reshape(1, d_inner), w_out, final_norm_w.reshape(1, d))


def _hgrn_lower_bounds(lb_logits):
    p = jax.nn.softmax(lb_logits.astype(F32), axis=0)
    return jnp.cumsum(p, axis=0) - p[0]


def _q_bound(norm_w, w_q):
    d = w_q.shape[0]
    col_norm = jnp.sqrt(jnp.sum(w_q.astype(F32) ** 2, axis=0))
    return (Q_BOUND_SLACK * (d ** 0.5) * (HEAD_DIM ** -0.5)) * jnp.max(jnp.abs(norm_w)) * col_norm


def kernel(x, norm_w, final_norm_w, conv_w_in, conv_kernel, conv_w_out,
           hgrn_w_in, hgrn_lb_logits, hgrn_norm_w, hgrn_w_out):
    depth = norm_w.shape[0]
    d_inner = hgrn_w_out.shape[1]
    n_heads = d_inner // HEAD_DIM
    lower_bounds = _hgrn_lower_bounds(hgrn_lb_logits)
    conv_w_in_b = conv_w_in.astype(BF16)
    conv_w_out_b = conv_w_out.astype(BF16)
    hgrn_w_qffi_b = hgrn_w_in[:, :, :4 * d_inner].astype(BF16)
    hgrn_w_z_b = hgrn_w_in[:, :, 4 * d_inner:].astype(BF16)
    hgrn_w_out_b = hgrn_w_out.astype(BF16)
    assert depth % 2 == 0
    for layer in range(depth):
        j = layer // 2
        if layer % 2 == 0:
            x = _conv_layer(x, norm_w[layer], conv_w_in_b[j], conv_kernel[j], conv_w_out_b[j])
        else:
            lbc = jnp.clip(lower_bounds[j], 0.0, 1.0 - 1e-6).reshape(1, d_inner)
            lbf = jnp.maximum(lbc, LB_FLOOR)
            kadd = (lbf - lbc).reshape(n_heads, 1, HEAD_DIM)
            qbound = _q_bound(norm_w[layer], hgrn_w_in[j, :, :d_inner]).reshape(n_heads, 1, HEAD_DIM)
            q, v, lff, lfb = _hgrn_in(x, norm_w[layer], hgrn_w_qffi_b[j], lbc, lbf)
            o = _hgrn_scan(q, v, lff, lfb, kadd, qbound)
            x = _hgrn_out(x, o, norm_w[layer], hgrn_w_z_b[j], hgrn_norm_w[j], hgrn_w_out_b[j],
                          final_norm_w, final_norm=(layer == depth - 1))
    return x
```

```python
import functools

import numpy as np
import jax
import jax.numpy as jnp
from jax import lax
from jax.experimental import pallas as pl
from jax.experimental.pallas import tpu as pltpu

HEAD_DIM = 128
EPS = 1e-6
LB_FLOOR = 1e-30
CONV_WIDTH = 3

V7X_LANES = 128
V7X_BF16_SUBLANES = 16
VMEM_LIMIT_BYTES = 56 * 1024 * 1024

SCAN_CHUNK = 64
SCAN_LEVELS = (32, 16, 8, 4, 2, 1)
SCAN_BCAST_LEVELS = 4
SCAN_FAST_RANGE = 86.0
SCAN_FAST_QMAX = 2.0 ** 20
Q_BOUND_SLACK = 1.02
SCAN_FAST_UNROLL = 4
LOG2_E = 1.4426950408889634

F32 = jnp.float32
BF16 = jnp.bfloat16


def _dot(a, b):
    return jnp.dot(a, b, preferred_element_type=F32)


def _dot_nt(a, b):
    return lax.dot_general(a, b, (((1,), (1,)), ((), ())), preferred_element_type=F32)


def _dot_tn(a, b):
    return lax.dot_general(a, b, (((0,), (0,)), ((), ())), preferred_element_type=F32)


def _rms_norm(xv, w):
    ms = jnp.mean(xv * xv, axis=-1, keepdims=True)
    return xv * lax.rsqrt(ms + EPS) * w


def _silu(z):
    return z * jax.nn.sigmoid(z)


def _const_spec(shape):
    nd = len(shape)
    return pl.BlockSpec(shape, lambda *_: (0,) * nd, pipeline_mode=pl.Buffered(1))


def _conv_layer_kernel(x_ref, xp_ref, xn_ref, nw_ref, win_ref, ck_ref, wout_ref, o_ref, h_scr,
                       *, tt, te, halo):
    i = pl.program_id(1)
    nt = pl.num_programs(1)
    d_inner = wout_ref.shape[0]
    nw = nw_ref[...]
    x = x_ref[0]
    h_scr[halo:halo + tt, :] = _rms_norm(x, nw).astype(BF16)
    h_scr[0:halo, :] = jnp.where(i > 0, _rms_norm(xp_ref[0], nw), 0.0).astype(BF16)
    h_scr[halo + tt:, :] = jnp.where(i < nt - 1, _rms_norm(xn_ref[0], nw), 0.0).astype(BF16)

    rows = tt + 2 * halo
    acc = jnp.zeros((tt, o_ref.shape[-1]), F32)
    for e in range(d_inner // te):
        def cols(g):
            return slice(g * d_inner + e * te, g * d_inner + (e + 1) * te)
        h_main = h_scr[halo:halo + tt, :]
        h_ext = h_scr[...]
        b_gate = _dot(h_main, win_ref[:, cols(0)])
        v = _dot(h_ext, win_ref[:, cols(1)]) * _dot(h_ext, win_ref[:, cols(2)])
        z = _dot(h_main, win_ref[:, cols(3)])
        ck = ck_ref[:, e * te:(e + 1) * te]
        v_prev = pltpu.roll(v, 1, 0)[halo:halo + tt]
        v_next = pltpu.roll(v, rows - 1, 0)[halo:halo + tt]
        conv = ck[0:1] * v_prev + ck[1:2] * v[halo:halo + tt] + ck[2:3] * v_next
        y = b_gate * conv * _silu(z)
        acc = acc + _dot(y.astype(BF16), wout_ref[e * te:(e + 1) * te, :])
    o_ref[0] = x + acc


def _conv_layer(x, norm_w, w_in, conv_k, w_out, *, tt=1024, te=512):
    bsz, seq, d = x.shape
    d_inner = w_out.shape[0]
    halo = V7X_BF16_SUBLANES
    assert seq % tt == 0 and tt % halo == 0 and d_inner % te == 0
    nt = seq // tt
    hb = tt // halo
    last_hb = seq // halo - 1
    kern = functools.partial(_conv_layer_kernel, tt=tt, te=te, halo=halo)
    return pl.pallas_call(
        kern,
        name="conv_layer",
        grid=(bsz, nt),
        in_specs=[
            pl.BlockSpec((1, tt, d), lambda b, i: (b, i, 0)),
            pl.BlockSpec((1, halo, d), lambda b, i: (b, jnp.maximum(i * hb - 1, 0), 0)),
            pl.BlockSpec((1, halo, d), lambda b, i: (b, jnp.minimum((i + 1) * hb, last_hb), 0)),
            _const_spec((1, d)),
            _const_spec(w_in.shape),
            _const_spec(conv_k.shape),
            _const_spec(w_out.shape),
        ],
        out_specs=pl.BlockSpec((1, tt, d), lambda b, i: (b, i, 0)),
        out_shape=jax.ShapeDtypeStruct(x.shape, F32),
        scratch_shapes=[pltpu.VMEM((tt + 2 * halo, d), BF16)],
        compiler_params=pltpu.CompilerParams(
            dimension_semantics=("parallel", "arbitrary"), vmem_limit_bytes=VMEM_LIMIT_BYTES),
    )(x, x, x, norm_w.reshape(1, d), w_in, conv_k, w_out)


def _hgrn_in_kernel(x_ref, nw_ref, win_ref, lbc_ref, lbf_ref, q_ref, v_ref, lff_ref, lfb_ref, *, cw):
    d_inner = lbc_ref.shape[-1]
    h = _rms_norm(x_ref[0], nw_ref[...]).astype(BF16)
    hpc = cw // HEAD_DIM

    def proj(g, j):
        return _dot(h, win_ref[:, g * d_inner + j * cw:g * d_inner + (j + 1) * cw])

    def put_heads(ref, j, val):
        for i in range(hpc):
            ref[0, j * hpc + i] = val[:, i * HEAD_DIM:(i + 1) * HEAD_DIM].astype(ref.dtype)

    def log2_forget(fp, j):
        one_minus_lb = 1.0 - lbc_ref[:, j * cw:(j + 1) * cw]
        lb_floor = lbf_ref[:, j * cw:(j + 1) * cw]
        sig = 1.0 / (1.0 + jnp.exp(-fp))
        return jnp.log(lb_floor + one_minus_lb * sig) * LOG2_E

    for j in range(d_inner // cw):
        fp_fw = proj(1, j)
        put_heads(q_ref, j, proj(0, j) * (HEAD_DIM ** -0.5))
        put_heads(lff_ref, j, log2_forget(fp_fw, j))
        fp_bw = proj(2, j)
        put_heads(v_ref, j, proj(3, j))
        put_heads(lfb_ref, j, log2_forget(fp_bw, j))


def _hgrn_in(x, norm_w, w_qffi, lbc, lbf, *, tt=512, cw=256):
    bsz, seq, d = x.shape
    d_inner = lbc.shape[-1]
    n_heads = d_inner // HEAD_DIM
    assert seq % tt == 0 and d_inner % cw == 0 and cw % HEAD_DIM == 0 and w_qffi.shape == (d, 4 * d_inner)
    head_shape = (bsz, n_heads, seq, HEAD_DIM)
    head_spec = pl.BlockSpec((1, n_heads, tt, HEAD_DIM), lambda b, i: (b, 0, i, 0))
    return pl.pallas_call(
        functools.partial(_hgrn_in_kernel, cw=cw),
        name="hgrn_in",
        grid=(bsz, seq // tt),
        in_specs=[
            pl.BlockSpec((1, tt, d), lambda b, i: (b, i, 0)),
            _const_spec((1, d)),
            _const_spec(w_qffi.shape),
            _const_spec((1, d_inner)),
            _const_spec((1, d_inner)),
        ],
        out_specs=[head_spec] * 4,
        out_shape=[
            jax.ShapeDtypeStruct(head_shape, BF16),
            jax.ShapeDtypeStruct(head_shape, BF16),
            jax.ShapeDtypeStruct(head_shape, F32),
            jax.ShapeDtypeStruct(head_shape, F32),
        ],
        compiler_params=pltpu.CompilerParams(
            dimension_semantics=("parallel", "parallel"), vmem_limit_bytes=VMEM_LIMIT_BYTES),
    )(x, norm_w.reshape(1, d), w_qffi, lbc, lbf)


def _scan_constants():
    c = SCAN_CHUNK
    levels = SCAN_LEVELS
    nb = SCAN_BCAST_LEVELS
    tri = np.zeros((2, c, c), np.float32)
    qsel = np.zeros((2, len(levels), c, 1), np.float32)
    mask = np.zeros((2, len(levels) + 1, c, c), np.float32)
    coef = np.zeros((2, len(levels) - nb, 3, c, 1), np.float32)
    refs = [[[] for _ in range(nb)] for _ in range(2)]
    flip = lambda a: a[::-1, ::-1]
    tri[0] = np.tril(np.ones((c, c), np.float32))
    tri[1] = flip(tri[0])
    mask[0, 0] = mask[1, 0] = np.eye(c, dtype=np.float32)
    for li, m in enumerate(levels):
        expo = np.zeros((c, c), np.float32)
        lmask = np.zeros((c, c), np.float32)
        late = np.zeros((c,), np.float32)
        for r in range(c):
            blk, pos = divmod(r, 2 * m)
            ref = blk * 2 * m + m - 1
            if pos >= m:
                late[r] = 1.0
                expo[r, ref + 1:r + 1] = 1.0
                lmask[r, blk * 2 * m:blk * 2 * m + m] = 1.0
            else:
                expo[r, r + 1:ref + 1] = 1.0
        for d in range(2):
            ex = expo if d == 0 else flip(expo)
            qsel[d, li, :, 0] = late if d == 0 else late[::-1]
            mask[d, li + 1] = lmask if d == 0 else flip(lmask)
            if li < nb:
                for blk in range(c // (2 * m)):
                    ref = blk * 2 * m + m - 1
                    refs[d][li].append(ref if d == 0 else c - 1 - ref)
                if d == 1:
                    refs[d][li] = sorted(refs[d][li])
            else:
                tridiag = np.zeros_like(ex)
                for r in range(c):
                    for j, off in enumerate((-1, 0, 1)):
                        if 0 <= r + off < c:
                            coef[d, li - nb, j, r, 0] = ex[r, r + off]
                            tridiag[r, r + off] = ex[r, r + off]
                assert np.array_equal(tridiag, ex)
    lanes = lambda a: np.ascontiguousarray(np.broadcast_to(a, a.shape[:-1] + (HEAD_DIM,)))
    return tri, lanes(qsel), mask, lanes(coef), refs


def _chunk_cumsum(d, lf, tri_ref):
    hi = lf.astype(BF16)
    r1 = lf - hi.astype(F32)
    mid = r1.astype(BF16)
    lo = (r1 - mid.astype(F32)).astype(BF16)
    g3 = _dot(tri_ref[d], jnp.concatenate([hi, mid, lo], axis=1))
    return (g3[:, :HEAD_DIM] + g3[:, HEAD_DIM:2 * HEAD_DIM]) + g3[:, 2 * HEAD_DIM:]


def _key_from_log2f(lf, kadd):
    return (1.0 - jnp.exp2(lf)) + kadd


def _emit(acc_ref, o_ref, rows, out, final):
    if final:
        o_ref[rows, :] = (acc_ref[rows, :] + out).astype(o_ref.dtype)
    else:
        acc_ref[rows, :] = out


def _scan_pair_fast(chains, tri2_ref, causal2_ref, vsel_ref, final, worst):
    c = SCAN_CHUNK
    half = c // 2
    dh = HEAD_DIM

    def per_chunk(fn):
        return jnp.concatenate([jnp.broadcast_to(fn(j), (c, dh)) for j in range(2)], axis=0)

    rows = [pl.ds(ch[8], 2 * c) for ch in chains]
    lf_l = [ch[4][rw, :] for ch, rw in zip(chains, rows)]
    g2 = []
    for ch, lf in zip(chains, lf_l):
        hi = lf.astype(BF16)
        mid = (lf - hi.astype(F32)).astype(BF16)
        g2.append(_dot(tri2_ref[ch[0]], jnp.concatenate([hi, mid], axis=1)))
    a_l, b_l, qg_l, kg_l, dcol_l = [], [], [], [], []
    for ch, rw, lf, gg in zip(chains, rows, lf_l, g2):
        d = ch[0]
        g = gg[:, :dh] + gg[:, dh:]
        mid_row = (lambda j: j * c + half - 1) if d == 0 else (lambda j: j * c + half)
        last_row = (lambda j: j * c + c - 1) if d == 0 else (lambda j: j * c)
        r_row = lambda j: g[mid_row(j):mid_row(j) + 1, :]
        gl_row = lambda j: g[last_row(j):last_row(j) + 1, :]
        for j in range(2):
            worst = jnp.minimum(worst, jnp.minimum(r_row(j), gl_row(j) - r_row(j)))
        dg = g - per_chunk(r_row)
        a = ch[1][rw, :] * jnp.exp2(dg).astype(BF16)
        b = _key_from_log2f(lf, ch[2]).astype(BF16) * jnp.exp2(-dg).astype(BF16)
        a_l.append(a)
        b_l.append(b)
        qg_l.append(a * per_chunk(lambda j: jnp.exp2(r_row(j))).astype(BF16))
        kg_l.append(b * per_chunk(lambda j: jnp.exp2(gl_row(j) - r_row(j))).astype(BF16))
        dcol_l.append([jnp.broadcast_to(jnp.exp2(gl_row(j)), (dh, dh)).T for j in range(2)])
    v_l = [ch[3][rw, :] for ch, rw in zip(chains, rows)]
    p_l = [_dot_nt(a, b) for a, b in zip(a_l, b_l)]
    u_l = [_dot_tn(kg, jnp.concatenate([v, v], axis=1) * vsel_ref[...]) for kg, v in zip(kg_l, v_l)]
    for ch, rw, p, u, qg, v, dcol in zip(chains, rows, p_l, u_l, qg_l, v_l, dcol_l):
        d = ch[0]
        sc = jnp.where(causal2_ref[d] > 0.5, p, 0.0).astype(BF16)
        s0 = ch[5][...]
        first, second = (0, 1) if d == 0 else (1, 0)
        s1 = dcol[first] * s0 + u[:, first * dh:(first + 1) * dh]
        ch[5][...] = dcol[second] * s1 + u[:, second * dh:(second + 1) * dh]
        s_by_chunk = (s0, s1) if d == 0 else (s1, s0)
        rhs = jnp.concatenate([
            jnp.concatenate([s_by_chunk[0].astype(BF16), s_by_chunk[1].astype(BF16)], axis=1),
            jnp.concatenate([v, v], axis=1)], axis=0)
        o2 = _dot(jnp.concatenate([qg, sc], axis=1), rhs)
        _emit(ch[6], ch[7], rw, jnp.concatenate([o2[:c, :dh], o2[c:, dh:]], axis=0), final)
    return worst


def _scan_chunk(d, refs, q_ref, kadd, v_ref, lf_ref, tri_ref, qsel_ref, mask_ref, coef_ref,
                s_ref, acc_ref, o_ref, row0, final):
    c = SCAN_CHUNK
    nb = SCAN_BCAST_LEVELS
    rows = pl.ds(row0, c)
    q_b = q_ref[rows, :]
    v_b = v_ref[rows, :]
    lf = lf_ref[rows, :]
    q = q_b.astype(F32)
    k = _key_from_log2f(lf, kadd)
    g = _chunk_cumsum(d, lf, tri_ref)
    g_last = g[c - 1:c, :] if d == 0 else g[0:1, :]

    state = s_ref[...]
    qg = (q * jnp.exp2(g)).astype(BF16)
    out = _dot(qg, state.astype(BF16))

    scores = mask_ref[d, 0] * _dot_nt(q_b, k.astype(BF16))
    lf_dn = pltpu.roll(lf, 1, 0)
    lf_up = pltpu.roll(lf, c - 1, 0)
    for li, m in enumerate(SCAN_LEVELS):
        late = qsel_ref[d, li] > 0.5
        if li < nb:
            g_blk = jnp.concatenate(
                [jnp.broadcast_to(g[r:r + 1, :], (2 * m, HEAD_DIM)) for r in refs[d][li]], axis=0)
            expo = jnp.where(late, g - g_blk, g_blk - g)
        else:
            cf = coef_ref[d, li - nb]
            expo = cf[0] * lf_dn + cf[1] * lf + cf[2] * lf_up
        xx = (jnp.where(late, q, k) * jnp.exp2(expo)).astype(BF16)
        scores = scores + mask_ref[d, li + 1] * _dot_nt(xx, xx)
    out = out + _dot(scores.astype(BF16), v_b)
    _emit(acc_ref, o_ref, rows, out, final)

    kg = (k * jnp.exp2(g_last - g)).astype(BF16)
    decay = jnp.broadcast_to(jnp.exp2(g_last), (HEAD_DIM, HEAD_DIM)).T
    s_ref[...] = decay * state + _dot_tn(kg, v_b)


def _hgrn_scan_kernel(q_ref, v_ref, lff_ref, lfb_ref, kadd_ref, qbound_ref,
                      tri_ref, tri2_ref, causal2_ref, vsel_ref, qsel_ref, mask_ref, coef_ref, o_ref,
                      s_scr, acc_scr, *, hp, refs):
    seq = q_ref.shape[2]
    c = SCAN_CHUNK
    nc = seq // c
    n_pairs = nc // 2
    n_trips = n_pairs // SCAN_FAST_UNROLL
    s_scr[...] = jnp.zeros(s_scr.shape, F32)

    def fast_body(final, ti, carry):
        chains = []
        for j in range(SCAN_FAST_UNROLL):
            pi = ti * SCAN_FAST_UNROLL + j
            row_f = pl.multiple_of(pi * 2 * c, 2 * c)
            row_b = pl.multiple_of((n_pairs - 1 - pi) * 2 * c, 2 * c)
            for hd in range(hp):
                chains.append((0, q_ref.at[0, hd], kadd_ref[hd], v_ref.at[0, hd], lff_ref.at[0, hd],
                               s_scr.at[hd], acc_scr.at[hd], o_ref.at[0, hd], row_f))
                chains.append((1, q_ref.at[0, hd], kadd_ref[hd], v_ref.at[0, hd], lfb_ref.at[0, hd],
                               s_scr.at[hp + hd], acc_scr.at[hd], o_ref.at[0, hd], row_b))
        return _scan_pair_fast(chains, tri2_ref, causal2_ref, vsel_ref, final, carry)

    def robust_body(final, ci, carry):
        row_f = pl.multiple_of(ci * c, c)
        row_b = pl.multiple_of((nc - 1 - ci) * c, c)
        for hd in range(hp):
            _scan_chunk(0, refs, q_ref.at[0, hd], kadd_ref[hd], v_ref.at[0, hd], lff_ref.at[0, hd],
                        tri_ref, qsel_ref, mask_ref, coef_ref,
                        s_scr.at[hd], acc_scr.at[hd], o_ref.at[0, hd], row_f, final)
            _scan_chunk(1, refs, q_ref.at[0, hd], kadd_ref[hd], v_ref.at[0, hd], lfb_ref.at[0, hd],
                        tri_ref, qsel_ref, mask_ref, coef_ref,
                        s_scr.at[hp + hd], acc_scr.at[hd], o_ref.at[0, hd], row_b, final)
        return carry

    worst = jnp.zeros((1, HEAD_DIM), F32)
    worst = lax.fori_loop(0, n_trips // 2, functools.partial(fast_body, False), worst)
    worst = lax.fori_loop(n_trips // 2, n_trips, functools.partial(fast_body, True), worst)
    fast_ok = jnp.logical_and(jnp.min(worst) >= -SCAN_FAST_RANGE,
                              jnp.max(qbound_ref[...]) <= SCAN_FAST_QMAX)

    @pl.when(jnp.logical_not(fast_ok))
    def _():
        s_scr[...] = jnp.zeros(s_scr.shape, F32)
        lax.fori_loop(0, nc // 2, functools.partial(robust_body, False), 0)
        lax.fori_loop(nc // 2, nc, functools.partial(robust_body, True), 0)


def _hgrn_scan(q, v, lff, lfb, kadd, qbound, *, hp=2):
    bsz, n_heads, seq, dh = q.shape
    assert dh == HEAD_DIM and n_heads % hp == 0
    assert seq % (4 * SCAN_CHUNK * SCAN_FAST_UNROLL) == 0
    tri, qsel, mask, coef, refs = _scan_constants()
    c = SCAN_CHUNK
    tri2 = np.zeros((2, 2 * c, 2 * c), np.float32)
    tri2[:, :c, :c] = tri
    tri2[:, c:, c:] = tri
    vsel = np.zeros((2 * c, 2 * dh), np.float32)
    vsel[:c, :dh] = 1.0
    vsel[c:, dh:] = 1.0
    consts = (jnp.asarray(tri, BF16), jnp.asarray(tri2, BF16), jnp.asarray(tri2), jnp.asarray(vsel, BF16),
              jnp.asarray(qsel), jnp.asarray(mask), jnp.asarray(coef))
    head_spec = pl.BlockSpec((1, hp, seq, dh), lambda b, h: (b, h, 0, 0))
    kern = functools.partial(_hgrn_scan_kernel, hp=hp, refs=refs)
    return pl.pallas_call(
        kern,
        name="hgrn_scan",
        grid=(bsz, n_heads // hp),
        in_specs=[head_spec] * 4 + [pl.BlockSpec((hp, 1, dh), lambda b, h: (h, 0, 0))] * 2
        + [_const_spec(a.shape) for a in consts],
        out_specs=head_spec,
        out_shape=jax.ShapeDtypeStruct(q.shape, BF16),
        scratch_shapes=[
            pltpu.VMEM((2 * hp, dh, dh), F32),
            pltpu.VMEM((hp, seq, dh), F32),
        ],
        compiler_params=pltpu.CompilerParams(
            dimension_semantics=("parallel", "parallel"), vmem_limit_bytes=VMEM_LIMIT_BYTES),
    )(q, v, lff, lfb, kadd, qbound, *consts)


def _hgrn_out_kernel(x_ref, o_ref, nw_ref, wz_ref, hnw_ref, wout_ref, fnw_ref, out_ref, *, final_norm):
    n_heads = o_ref.shape[1]
    x = x_ref[0]
    z = _dot(_rms_norm(x, nw_ref[...]).astype(BF16), wz_ref[...])
    heads = []
    for hd in range(n_heads):
        o = o_ref[0, hd].astype(F32)
        heads.append(o * lax.rsqrt(jnp.mean(o * o, axis=-1, keepdims=True) + EPS))
    y = jnp.concatenate(heads, axis=-1) * hnw_ref[...] * _silu(z)
    res = x + _dot(y.astype(BF16), wout_ref[...])
    if final_norm:
        res = _rms_norm(res, fnw_ref[...])
    out_ref[0] = res


def _hgrn_out(x, o, norm_w, w_z, head_norm_w, w_out, final_norm_w, *, final_norm, tt=1024):
    bsz, seq, d = x.shape
    n_heads = o.shape[1]
    d_inner = w_out.shape[0]
    assert seq % tt == 0 and w_z.shape == (d, d_inner)
    kern = functools.partial(_hgrn_out_kernel, final_norm=final_norm)
    return pl.pallas_call(
        kern,
        name="hgrn_out",
        grid=(bsz, seq // tt),
        in_specs=[
            pl.BlockSpec((1, tt, d), lambda b, i: (b, i, 0)),
            pl.BlockSpec((1, n_heads, tt, HEAD_DIM), lambda b, i: (b, 0, i, 0)),
            _const_spec((1, d)),
            _const_spec(w_z.shape),
            _const_spec((1, d_inner)),
            _const_spec(w_out.shape),
            _const_spec((1, d)),
        ],
        out_specs=pl.BlockSpec((1, tt, d), lambda b, i: (b, i, 0)),
        out_shape=jax.ShapeDtypeStruct(x.shape, F32),
        compiler_params=pltpu.CompilerParams(
            dimension_semantics=("parallel", "parallel"), vmem_limit_bytes=VMEM_LIMIT_BYTES),
    )(x, o, norm_w.reshape(1, d), w_z, head_norm_w.reshape(1, d_inner), w_out, final_norm_w.reshape(1, d))


def _hgrn_lower_bounds(lb_logits):
    p = jax.nn.softmax(lb_logits.astype(F32), axis=0)
    return jnp.cumsum(p, axis=0) - p[0]


def _q_bound(norm_w, w_q):
    d = w_q.shape[0]
    col_norm = jnp.sqrt(jnp.sum(w_q.astype(F32) ** 2, axis=0))
    return (Q_BOUND_SLACK * (d ** 0.5) * (HEAD_DIM ** -0.5)) * jnp.max(jnp.abs(norm_w)) * col_norm


def kernel(x, norm_w, final_norm_w, conv_w_in, conv_kernel, conv_w_out,
           hgrn_w_in, hgrn_lb_logits, hgrn_norm_w, hgrn_w_out):
    depth = norm_w.shape[0]
    d_inner = hgrn_w_out.shape[1]
    n_heads = d_inner // HEAD_DIM
    lower_bounds = _hgrn_lower_bounds(hgrn_lb_logits)
    conv_w_in_b = conv_w_in.astype(BF16)
    conv_w_out_b = conv_w_out.astype(BF16)
    hgrn_w_qffi_b = hgrn_w_in[:, :, :4 * d_inner].astype(BF16)
    hgrn_w_z_b = hgrn_w_in[:, :, 4 * d_inner:].astype(BF16)
    hgrn_w_out_b = hgrn_w_out.astype(BF16)
    assert depth % 2 == 0
    for layer in range(depth):
        j = layer // 2
        if layer % 2 == 0:
            x = _conv_layer(x, norm_w[layer], conv_w_in_b[j], conv_kernel[j], conv_w_out_b[j])
        else:
            lbc = jnp.clip(lower_bounds[j], 0.0, 1.0 - 1e-6).reshape(1, d_inner)
            lbf = jnp.maximum(lbc, LB_FLOOR)
            kadd = (lbf - lbc).reshape(n_heads, 1, HEAD_DIM)
            qbound = _q_bound(norm_w[layer], hgrn_w_in[j, :, :d_inner]).reshape(n_heads, 1, HEAD_DIM)
            q, v, lff, lfb = _hgrn_in(x, norm_w[layer], hgrn_w_qffi_b[j], lbc, lbf)
            o = _hgrn_scan(q, v, lff, lfb, kadd, qbound)
            x = _hgrn_out(x, o, norm_w[layer], hgrn_w_z_b[j], hgrn_norm_w[j], hgrn_w_out_b[j],
                          final_norm_w, final_norm=(layer == depth - 1))
    return x
```
